```python
import math, functools
import jax, jax.numpy as jnp
from jax import lax
import numpy as np

D_MODEL = 1024
BATCH = 32
SEQ = 2048
DEPTH = 1

CTX_LEN = 256
GRID_W = 64
M_HEADS = 4
M_WIDTH = D_MODEL
M_HEAD_DIM = M_WIDTH // M_HEADS
CHUNK = 64
H_WIDTH = D_MODEL
HY_ORDER = 2
HY_EMB = 33
HY_BANDS = (HY_EMB - 1) // 2
HY_FILTER_WIDTH = 64
DECAY_TARGET = 1e-2
FAST_DECAY_PCT = 0.3
SLOW_DECAY_PCT = 1.5
SHORT_CONV = 3
D_FF = ((8 * D_MODEL + 3 * 256 - 1) // (3 * 256)) * 256
EPS = 1e-6
NEG = -1e30

K_OFF = 0
V_OFF = M_WIDTH
G_OFF = 2 * M_WIDTH
Q_OFF = G_OFF + 4 * M_HEADS
O_OFF = Q_OFF + M_WIDTH
HY_OFF = O_OFF + M_WIDTH
MG_OFF = HY_OFF + 3 * H_WIDTH
N_IN = MG_OFF + 2 * D_MODEL
R_O = O_OFF - Q_OFF
R_HY = HY_OFF - Q_OFF
R_GM = MG_OFF - Q_OFF
R_GH = R_GM + D_MODEL

kernel_name = "hybrid_mlstm_hyena_diffusion_block"


def rmsnorm(x, g):
    xf = x.astype(jnp.float32)
    y = xf * lax.rsqrt(jnp.mean(xf * xf, axis=-1, keepdims=True) + EPS)
    return (y * g.astype(jnp.float32)).astype(x.dtype)


def seq_conv(u, w, b):
    ch = u.shape[-1]
    y = lax.conv_general_dilated(
        u, w.astype(u.dtype)[:, None, :], window_strides=(1,),
        padding=((SHORT_CONV // 2, SHORT_CONV // 2),),
        dimension_numbers=("NWC", "WIO", "NWC"), feature_group_count=ch)
    return y + b.astype(u.dtype)


def grid_conv(u, w, b, rows):
    bsz, n, ch = u.shape
    return seq_conv(u.reshape(bsz * rows, GRID_W, ch), w, b).reshape(bsz, n, ch)


def to_heads(u):
    bsz, n, _ = u.shape
    return u.reshape(bsz, n, M_HEADS, M_HEAD_DIM).transpose(0, 2, 1, 3)


def flip(u):
    return jnp.flip(u, axis=2)


def project_kvg(h, lp, conv_fn):
    f32 = jnp.float32
    p = h @ lp["w_in"][:, :Q_OFF] + lp["b_in"][:Q_OFF]
    k = jax.nn.silu(conv_fn(p[..., K_OFF:V_OFF], lp["kq_conv_w"][:, :M_WIDTH], lp["kq_conv_b"][:M_WIDTH]))
    kh = to_heads(k.astype(f32)) * (M_HEAD_DIM ** -0.5)
    vh = to_heads(p[..., V_OFF:G_OFF].astype(f32))
    bsz, n, _ = h.shape
    g = p[..., G_OFF:Q_OFF].astype(f32).reshape(bsz, n, 4, M_HEADS).transpose(2, 0, 3, 1)
    lg = (g[0], jax.nn.log_sigmoid(g[1]), g[2], jax.nn.log_sigmoid(g[3]))
    return kh, vh, lg


def mlstm_final_state(k, v, log_i, log_f):
    b = jnp.cumsum(log_f, axis=-1)
    b_end = b[..., -1]
    logw = b_end[..., None] - b + log_i
    m = jnp.maximum(b_end, jnp.max(logw, axis=-1))
    w = jnp.exp(logw - m[..., None])
    C = jnp.einsum("bhsv,bhsk->bhvk", w[..., None] * v, k)
    n = jnp.einsum("bhs,bhsk->bhk", w, k)
    return C, n, m


def mlstm_chunkwise(q, k, v, log_i, log_f, C0, n0, m0):
    bsz, nh, n, dh = q.shape
    nc = n // CHUNK

    def split(u):
        return jnp.moveaxis(u.reshape(u.shape[:2] + (nc, CHUNK) + u.shape[3:]), 2, 0)

    tril = jnp.tril(jnp.ones((CHUNK, CHUNK), dtype=bool))

    def step(carry, blk):
        C, nv, m = carry
        qb, kb, vb, ib, fb = blk
        b = jnp.cumsum(fb, axis=-1)
        logD = jnp.where(tril, b[..., :, None] - b[..., None, :] + ib[..., None, :], NEG)
        inter = b + m[..., None]
        m_t = jnp.maximum(inter, jnp.max(logD, axis=-1))
        s = jnp.einsum("bhtd,bhsd->bhts", qb, kb) * jnp.exp(logD - m_t[..., None])
        a = jnp.exp(inter - m_t)
        num = jnp.einsum("bhts,bhsd->bhtd", s, vb) + a[..., None] * jnp.einsum("bhvk,bhtk->bhtv", C, qb)
        den = jnp.sum(s, axis=-1) + a * jnp.einsum("bhk,bhtk->bht", nv, qb)
        hb = num / jnp.maximum(jnp.abs(den), jnp.exp(-m_t))[..., None]
        m_new = m_t[..., -1]
        w = jnp.exp(b[..., -1:] - b + ib - m_new[..., None])
        decay = jnp.exp(b[..., -1] + m - m_new)
        C = decay[..., None, None] * C + jnp.einsum("bhsv,bhsk->bhvk", w[..., None] * vb, kb)
        nv = decay[..., None] * nv + jnp.einsum("bhs,bhsk->bhk", w, kb)
        return (C, nv, m_new), hb

    _, hs = lax.scan(step, (C0, n0, m0), (split(q), split(k), split(v), split(log_i), split(log_f)))
    return jnp.moveaxis(hs, 0, 2).reshape(bsz, nh, n, dh)


def zero_state(bsz):
    f32 = jnp.float32
    return (jnp.zeros((bsz, M_HEADS, M_HEAD_DIM, M_HEAD_DIM), f32),
            jnp.zeros((bsz, M_HEADS, M_HEAD_DIM), f32),
            jnp.zeros((bsz, M_HEADS), f32))


def context_states(kh, vh, lg):
    li_f, lf_f, li_b, lf_b = lg
    st_f = mlstm_final_state(kh, vh, li_f, lf_f)
    st_b = mlstm_final_state(flip(kh), flip(vh), flip(li_b), flip(lf_b))
    return st_f, st_b


def head_norm(hh, g):
    mu = jnp.mean(hh, axis=-1, keepdims=True)
    var = jnp.mean(jnp.square(hh - mu), axis=-1, keepdims=True)
    y = (hh - mu) * lax.rsqrt(var + EPS)
    bsz, _, n, _ = hh.shape
    return y.transpose(0, 2, 1, 3).reshape(bsz, n, M_WIDTH) * g.astype(jnp.float32)


def hyena_spectra(n, lp):
    f32 = jnp.float32
    t = jnp.linspace(0.0, 1.0, n, dtype=f32)[:, None]
    bands = jnp.linspace(1e-4, HY_BANDS - 1, HY_BANDS, dtype=f32)
    ang = (2.0 * math.pi / n) * jnp.arange(n, dtype=f32)[:, None] * bands[None, :]
    z = jnp.concatenate([t, jnp.cos(ang), -jnp.sin(ang)], axis=-1)
    fr = lp["hy_freq"].astype(f32)
    a = jnp.sin(fr * (z @ lp["hy_w1"].astype(f32) + lp["hy_b1"].astype(f32)))
    a = jnp.sin(fr * (a @ lp["hy_w2"].astype(f32) + lp["hy_b2"].astype(f32)))
    a = jnp.sin(fr * (a @ lp["hy_w3"].astype(f32) + lp["hy_b3"].astype(f32)))
    window = jnp.exp(-t[:, :, None, None] * jnp.abs(lp["hy_decay"].astype(f32)))
    k = (a @ lp["hy_w4"].astype(f32)).reshape(n, HY_ORDER, 2, H_WIDTH) * window
    kf, kb = k[:, :, 0], k[:, :, 1]
    circ = jnp.concatenate([kf[:1] + kb[:1], kf[1:], jnp.zeros_like(kf[:1]), kb[:0:-1]], axis=0)
    return jnp.fft.rfft(jnp.moveaxis(circ, 1, 0), axis=1)


def fftconv(u, k_spec):
    n = u.shape[1]
    U = jnp.fft.rfft(u, n=2 * n, axis=1)
    return jnp.fft.irfft(U * k_spec, n=2 * n, axis=1)[:, :n]


def hyena(v, gates, spec, skip):
    z = v
    for o in range(HY_ORDER):
        z = gates[o] * (fftconv(z, spec[o]) + skip[o].astype(jnp.float32) * z)
    return z


def mixer(h, kh, vh, lg, st_f, st_b, lp, conv_fn, spec):
    f32 = jnp.float32
    p = h @ lp["w_in"][:, Q_OFF:] + lp["b_in"][Q_OFF:]
    q = jax.nn.silu(conv_fn(p[..., :R_O], lp["kq_conv_w"][:, M_WIDTH:], lp["kq_conv_b"][M_WIDTH:]))
    o = jax.nn.sigmoid(p[..., R_O:R_HY]).astype(f32)
    hy = conv_fn(p[..., R_HY:R_GM], lp["hy_conv_w"], lp["hy_conv_b"]).astype(f32)
    gm = jax.nn.sigmoid(p[..., R_GM:R_GH])
    gh = jax.nn.sigmoid(p[..., R_GH:])
    qh = to_heads(q.astype(f32))
    li_f, lf_f, li_b, lf_b = lg
    h_f = mlstm_chunkwise(qh, kh, vh, li_f, lf_f, *st_f)
    h_b = flip(mlstm_chunkwise(flip(qh), flip(kh), flip(vh), flip(li_b), flip(lf_b), *st_b))
    hm = head_norm(h_f + h_b, lp["m_norm_g"]) * o
    hv, hx1, hx2 = jnp.split(hy, 3, axis=-1)
    hh = hyena(hv, (hx1, hx2), spec, lp["hy_skip"])
    y = gm * (hm.astype(h.dtype) @ lp["w_pm"]) + gh * (hh.astype(h.dtype) @ lp["w_ph"])
    return y @ lp["w_out"]


def swiglu(h, w1, w3, w2):
    return (jax.nn.silu(h @ w1) * (h @ w3)) @ w2


def setup_inputs(seed: int = 0) -> dict:
    key = jax.random.key(seed)
    ks = iter(jax.random.split(key, 48))
    f32 = jnp.float32

    def nrm(shape, scale):
        return jax.random.normal(next(ks), shape, f32) * scale

    def uni(shape, lo, hi):
        return jax.random.uniform(next(ks), shape, f32, lo, hi)

    def gain(shape):
        return 1.0 + nrm(shape, 0.02)

    L = DEPTH
    D = D_MODEL
    b_in = jnp.concatenate([
        nrm((L, G_OFF), 0.02),
        nrm((L, M_HEADS), 0.1), uni((L, M_HEADS), 3.0, 6.0),
        nrm((L, M_HEADS), 0.1), uni((L, M_HEADS), 3.0, 6.0),
        nrm((L, N_IN - Q_OFF), 0.02)], axis=-1)
    decay_lo = -math.log(DECAY_TARGET) / SLOW_DECAY_PCT
    decay_hi = -math.log(DECAY_TARGET) / FAST_DECAY_PCT
    return {
        "x": nrm((BATCH, SEQ, D), 1.0),
        "c": nrm((BATCH, D), 1.0),
        "ctx": nrm((BATCH, CTX_LEN, D), 1.0),
        "c_ctx": nrm((D,), 1.0),
        "w_ada": nrm((L, D, 6 * D), 0.5 * D ** -0.5),
        "b_ada": nrm((L, 6 * D), 0.02),
        "norm1_g": gain((L, D)),
        "w_in": nrm((L, D, N_IN), D ** -0.5),
        "b_in": b_in,
        "kq_conv_w": nrm((L, SHORT_CONV, 2 * M_WIDTH), SHORT_CONV ** -0.5),
        "kq_conv_b": nrm((L, 2 * M_WIDTH), 0.02),
        "m_norm_g": gain((L, M_WIDTH)),
        "hy_conv_w": nrm((L, SHORT_CONV, 3 * H_WIDTH), SHORT_CONV ** -0.5),
        "hy_conv_b": nrm((L, 3 * H_WIDTH), 0.02),
        "hy_w1": nrm((L, HY_EMB, HY_FILTER_WIDTH), HY_EMB ** -0.5),
        "hy_b1": nrm((L, HY_FILTER_WIDTH), 0.02),
        "hy_w2": nrm((L, HY_FILTER_WIDTH, HY_FILTER_WIDTH), HY_FILTER_WIDTH ** -0.5),
        "hy_b2": nrm((L, HY_FILTER_WIDTH), 0.02),
        "hy_w3": nrm((L, HY_FILTER_WIDTH, HY_FILTER_WIDTH), HY_FILTER_WIDTH ** -0.5),
        "hy_b3": nrm((L, HY_FILTER_WIDTH), 0.02),
        "hy_w4": nrm((L, HY_FILTER_WIDTH, HY_ORDER * 2 * H_WIDTH), 0.05 * HY_FILTER_WIDTH ** -0.5),
        "hy_freq": 1.0 + nrm((L, HY_FILTER_WIDTH), 0.1),
        "hy_decay": uni((L, HY_ORDER, 2, H_WIDTH), decay_lo, decay_hi),
        "hy_skip": nrm((L, HY_ORDER, H_WIDTH), 0.5),
        "w_pm": nrm((L, M_WIDTH, D), M_WIDTH ** -0.5),
        "w_ph": nrm((L, H_WIDTH, D), H_WIDTH ** -0.5),
        "w_out": nrm((L, D, D), D ** -0.5),
        "norm2_g": gain((L, D)),
        "ffn_w1": nrm((L, D, D_FF), D ** -0.5),
        "ffn_w3": nrm((L, D, D_FF), D ** -0.5),
        "ffn_w2": nrm((L, D_FF, D), D_FF ** -0.5),
        "final_g": gain((D,)),
    }


def reference(x, c, ctx, c_ctx, w_ada, b_ada, norm1_g, w_in, b_in, kq_conv_w, kq_conv_b,
              m_norm_g, hy_conv_w, hy_conv_b, hy_w1, hy_b1, hy_w2, hy_b2, hy_w3, hy_b3,
              hy_w4, hy_freq, hy_decay, hy_skip, w_pm, w_ph, w_out, norm2_g,
              ffn_w1, ffn_w3, ffn_w2, final_g):
    rows = x.shape[1] // GRID_W
    lat_conv = functools.partial(grid_conv, rows=rows)
    seq_len = x.shape[1]
    ctx_len = ctx.shape[1]
    silu_c = jax.nn.silu(c)
    silu_cc = jax.nn.silu(c_ctx)
    ctx_s = ctx
    for l in range(DEPTH):
        last = l == DEPTH - 1
        lp = {
            "w_in": w_in[l], "b_in": b_in[l], "kq_conv_w": kq_conv_w[l], "kq_conv_b": kq_conv_b[l],
            "m_norm_g": m_norm_g[l], "hy_conv_w": hy_conv_w[l], "hy_conv_b": hy_conv_b[l],
            "hy_w1": hy_w1[l], "hy_b1": hy_b1[l], "hy_w2": hy_w2[l], "hy_b2": hy_b2[l],
            "hy_w3": hy_w3[l], "hy_b3": hy_b3[l], "hy_w4": hy_w4[l], "hy_freq": hy_freq[l],
            "hy_decay": hy_decay[l], "hy_skip": hy_skip[l],
            "w_pm": w_pm[l], "w_ph": w_ph[l], "w_out": w_out[l],
        }
        mod = (silu_c @ w_ada[l] + b_ada[l])[:, None, :]
        sh1, sc1, g1, sh2, sc2, g2 = jnp.split(mod, 6, axis=-1)
        modc = silu_cc @ w_ada[l] + b_ada[l]
        csh1, csc1, cg1, csh2, csc2, cg2 = jnp.split(modc, 6, axis=-1)
        hc = rmsnorm(ctx_s, norm1_g[l]) * (1.0 + csc1) + csh1
        khc, vhc, lgc = project_kvg(hc, lp, seq_conv)
        st_f, st_b = context_states(khc, vhc, lgc)
        h = rmsnorm(x, norm1_g[l]) * (1.0 + sc1) + sh1
        kh, vh, lg = project_kvg(h, lp, lat_conv)
        y = mixer(h, kh, vh, lg, st_f, st_b, lp, lat_conv, hyena_spectra(seq_len, lp))
        if not last:
            zero = zero_state(ctx_s.shape[0])
            yc = mixer(hc, khc, vhc, lgc, zero, zero, lp, seq_conv, hyena_spectra(ctx_len, lp))
            ctx_s = ctx_s + cg1 * yc
            hc2 = rmsnorm(ctx_s, norm2_g[l]) * (1.0 + csc2) + csh2
            ctx_s = ctx_s + cg2 * swiglu(hc2, ffn_w1[l], ffn_w3[l], ffn_w2[l])
        x = x + g1 * y
        h2 = rmsnorm(x, norm2_g[l]) * (1.0 + sc2) + sh2
        x = x + g2 * swiglu(h2, ffn_w1[l], ffn_w3[l], ffn_w2[l])
    return rmsnorm(x, final_g)
```

```python
import math, functools
import jax, jax.numpy as jnp
from jax import lax
import numpy as np
from jax.experimental import pallas as pl
from jax.experimental.pallas import tpu as pltpu

D_MODEL = 1024
BATCH = 32
SEQ = 2048
DEPTH = 1

CTX_LEN = 256
GRID_W = 64
M_HEADS = 4
M_WIDTH = D_MODEL
M_HEAD_DIM = M_WIDTH // M_HEADS
CHUNK = 64
H_WIDTH = D_MODEL
HY_ORDER = 2
HY_EMB = 33
HY_BANDS = (HY_EMB - 1) // 2
HY_FILTER_WIDTH = 64
SHORT_CONV = 3
D_FF = ((8 * D_MODEL + 3 * 256 - 1) // (3 * 256)) * 256
EPS = 1e-6
NEG = -1e30

K_OFF = 0
V_OFF = M_WIDTH
G_OFF = 2 * M_WIDTH
Q_OFF = G_OFF + 4 * M_HEADS
O_OFF = Q_OFF + M_WIDTH
HY_OFF = O_OFF + M_WIDTH
MG_OFF = HY_OFF + 3 * H_WIDTH
N_IN = MG_OFF + 2 * D_MODEL
R_O = O_OFF - Q_OFF
R_HY = HY_OFF - Q_OFF
R_GM = MG_OFF - Q_OFF
R_GH = R_GM + D_MODEL


def rmsnorm(x, g):
    xf = x.astype(jnp.float32)
    y = xf * lax.rsqrt(jnp.mean(xf * xf, axis=-1, keepdims=True) + EPS)
    return (y * g.astype(jnp.float32)).astype(x.dtype)


def seq_conv(u, w, b):
    ch = u.shape[-1]
    y = lax.conv_general_dilated(
        u, w.astype(u.dtype)[:, None, :], window_strides=(1,),
        padding=((SHORT_CONV // 2, SHORT_CONV // 2),),
        dimension_numbers=("NWC", "WIO", "NWC"), feature_group_count=ch)
    return y + b.astype(u.dtype)


def grid_conv(u, w, b, rows):
    bsz, n, ch = u.shape
    return seq_conv(u.reshape(bsz * rows, GRID_W, ch), w, b).reshape(bsz, n, ch)


def to_heads(u):
    bsz, n, _ = u.shape
    return u.reshape(bsz, n, M_HEADS, M_HEAD_DIM).transpose(0, 2, 1, 3)


def flip(u):
    return jnp.flip(u, axis=2)


def project_kvg(h, lp, conv_fn):
    f32 = jnp.float32
    p = h @ lp["w_in"][:, :Q_OFF] + lp["b_in"][:Q_OFF]
    k = jax.nn.silu(conv_fn(p[..., K_OFF:V_OFF], lp["kq_conv_w"][:, :M_WIDTH], lp["kq_conv_b"][:M_WIDTH]))
    kh = to_heads(k.astype(f32)) * (M_HEAD_DIM ** -0.5)
    vh = to_heads(p[..., V_OFF:G_OFF].astype(f32))
    bsz, n, _ = h.shape
    g = p[..., G_OFF:Q_OFF].astype(f32).reshape(bsz, n, 4, M_HEADS).transpose(2, 0, 3, 1)
    lg = (g[0], jax.nn.log_sigmoid(g[1]), g[2], jax.nn.log_sigmoid(g[3]))
    return kh, vh, lg


def mlstm_final_state(k, v, log_i, log_f):
    b = jnp.cumsum(log_f, axis=-1)
    b_end = b[..., -1]
    logw = b_end[..., None] - b + log_i
    m = jnp.maximum(b_end, jnp.max(logw, axis=-1))
    w = jnp.exp(logw - m[..., None])
    C = jnp.einsum("bhsv,bhsk->bhvk", w[..., None] * v, k)
    n = jnp.einsum("bhs,bhsk->bhk", w, k)
    return C, n, m


def mlstm_chunkwise(q, k, v, log_i, log_f, C0, n0, m0):
    bsz, nh, n, dh = q.shape
    nc = n // CHUNK

    def split(u):
        return jnp.moveaxis(u.reshape(u.shape[:2] + (nc, CHUNK) + u.shape[3:]), 2, 0)

    tril = jnp.tril(jnp.ones((CHUNK, CHUNK), dtype=bool))

    def step(carry, blk):
        C, nv, m = carry
        qb, kb, vb, ib, fb = blk
        b = jnp.cumsum(fb, axis=-1)
        logD = jnp.where(tril, b[..., :, None] - b[..., None, :] + ib[..., None, :], NEG)
        inter = b + m[..., None]
        m_t = jnp.maximum(inter, jnp.max(logD, axis=-1))
        s = jnp.einsum("bhtd,bhsd->bhts", qb, kb) * jnp.exp(logD - m_t[..., None])
        a = jnp.exp(inter - m_t)
        num = jnp.einsum("bhts,bhsd->bhtd", s, vb) + a[..., None] * jnp.einsum("bhvk,bhtk->bhtv", C, qb)
        den = jnp.sum(s, axis=-1) + a * jnp.einsum("bhk,bhtk->bht", nv, qb)
        hb = num / jnp.maximum(jnp.abs(den), jnp.exp(-m_t))[..., None]
        m_new = m_t[..., -1]
        w = jnp.exp(b[..., -1:] - b + ib - m_new[..., None])
        decay = jnp.exp(b[..., -1] + m - m_new)
        C = decay[..., None, None] * C + jnp.einsum("bhsv,bhsk->bhvk", w[..., None] * vb, kb)
        nv = decay[..., None] * nv + jnp.einsum("bhs,bhsk->bhk", w, kb)
        return (C, nv, m_new), hb

    _, hs = lax.scan(step, (C0, n0, m0), (split(q), split(k), split(v), split(log_i), split(log_f)))
    return jnp.moveaxis(hs, 0, 2).reshape(bsz, nh, n, dh)


def context_states(kh, vh, lg):
    li_f, lf_f, li_b, lf_b = lg
    st_f = mlstm_final_state(kh, vh, li_f, lf_f)
    st_b = mlstm_final_state(flip(kh), flip(vh), flip(li_b), flip(lf_b))
    return st_f, st_b


def head_norm(hh, g):
    mu = jnp.mean(hh, axis=-1, keepdims=True)
    var = jnp.mean(jnp.square(hh - mu), axis=-1, keepdims=True)
    y = (hh - mu) * lax.rsqrt(var + EPS)
    bsz, _, n, _ = hh.shape
    return y.transpose(0, 2, 1, 3).reshape(bsz, n, M_WIDTH) * g.astype(jnp.float32)


def hyena_spectra(n, lp):
    f32 = jnp.float32
    t = jnp.linspace(0.0, 1.0, n, dtype=f32)[:, None]
    bands = jnp.linspace(1e-4, HY_BANDS - 1, HY_BANDS, dtype=f32)
    ang = (2.0 * math.pi / n) * jnp.arange(n, dtype=f32)[:, None] * bands[None, :]
    z = jnp.concatenate([t, jnp.cos(ang), -jnp.sin(ang)], axis=-1)
    fr = lp["hy_freq"].astype(f32)
    a = jnp.sin(fr * (z @ lp["hy_w1"].astype(f32) + lp["hy_b1"].astype(f32)))
    a = jnp.sin(fr * (a @ lp["hy_w2"].astype(f32) + lp["hy_b2"].astype(f32)))
    a = jnp.sin(fr * (a @ lp["hy_w3"].astype(f32) + lp["hy_b3"].astype(f32)))
    window = jnp.exp(-t[:, :, None, None] * jnp.abs(lp["hy_decay"].astype(f32)))
    k = (a @ lp["hy_w4"].astype(f32)).reshape(n, HY_ORDER, 2, H_WIDTH) * window
    kf, kb = k[:, :, 0], k[:, :, 1]
    circ = jnp.concatenate([kf[:1] + kb[:1], kf[1:], jnp.zeros_like(kf[:1]), kb[:0:-1]], axis=0)
    return jnp.fft.rfft(jnp.moveaxis(circ, 1, 0), axis=1)


def fftconv(u, k_spec):
    n = u.shape[1]
    U = jnp.fft.rfft(u, n=2 * n, axis=1)
    return jnp.fft.irfft(U * k_spec, n=2 * n, axis=1)[:, :n]


def hyena(v, gates, spec, skip):
    z = v
    for o in range(HY_ORDER):
        z = gates[o] * (fftconv(z, spec[o]) + skip[o].astype(jnp.float32) * z)
    return z


def mixer(h, kh, vh, lg, st_f, st_b, lp, conv_fn, spec):
    f32 = jnp.float32
    p = h @ lp["w_in"][:, Q_OFF:] + lp["b_in"][Q_OFF:]
    q = jax.nn.silu(conv_fn(p[..., :R_O], lp["kq_conv_w"][:, M_WIDTH:], lp["kq_conv_b"][M_WIDTH:]))
    o = jax.nn.sigmoid(p[..., R_O:R_HY]).astype(f32)
    hy = conv_fn(p[..., R_HY:R_GM], lp["hy_conv_w"], lp["hy_conv_b"]).astype(f32)
    gm = jax.nn.sigmoid(p[..., R_GM:R_GH])
    gh = jax.nn.sigmoid(p[..., R_GH:])
    qh = to_heads(q.astype(f32))
    li_f, lf_f, li_b, lf_b = lg
    h_f = mlstm_chunkwise(qh, kh, vh, li_f, lf_f, *st_f)
    h_b = flip(mlstm_chunkwise(flip(qh), flip(kh), flip(vh), flip(li_b), flip(lf_b), *st_b))
    hm = head_norm(h_f + h_b, lp["m_norm_g"]) * o
    hv, hx1, hx2 = jnp.split(hy, 3, axis=-1)
    hh = hyena(hv, (hx1, hx2), spec, lp["hy_skip"])
    y = gm * (hm.astype(h.dtype) @ lp["w_pm"]) + gh * (hh.astype(h.dtype) @ lp["w_ph"])
    return y @ lp["w_out"]


def swiglu(h, w1, w3, w2):
    return (jax.nn.silu(h @ w1) * (h @ w3)) @ w2


def _fn_kernel(x_ref, g_ref, o_ref):
    x = x_ref[...]
    y = x * lax.rsqrt(jnp.mean(x * x, axis=-1, keepdims=True) + EPS)
    o_ref[...] = y * g_ref[...]


def _final_norm(x, g):
    b, n, d = x.shape
    x2 = x.reshape(b * n, d)
    out = pl.pallas_call(
        _fn_kernel,
        grid=(b * n // 1024,),
        in_specs=[pl.BlockSpec((1024, d), lambda i: (i, 0)), pl.BlockSpec((1, d), lambda i: (0, 0))],
        out_specs=pl.BlockSpec((1024, d), lambda i: (i, 0)),
        out_shape=jax.ShapeDtypeStruct((b * n, d), x.dtype),
    )(x2, g.reshape(1, d))
    return out.reshape(b, n, d)


def kernel(x, c, ctx, c_ctx, w_ada, b_ada, norm1_g, w_in, b_in, kq_conv_w, kq_conv_b, m_norm_g, hy_conv_w, hy_conv_b, hy_w1, hy_b1, hy_w2, hy_b2, hy_w3, hy_b3, hy_w4, hy_freq, hy_decay, hy_skip, w_pm, w_ph, w_out, norm2_g, ffn_w1, ffn_w3, ffn_w2, final_g):
    rows = x.shape[1] // GRID_W
    lat_conv = functools.partial(grid_conv, rows=rows)
    seq_len = x.shape[1]
    silu_c = jax.nn.silu(c)
    silu_cc = jax.nn.silu(c_ctx)
    l = 0
    lp = {
        "w_in": w_in[l], "b_in": b_in[l], "kq_conv_w": kq_conv_w[l], "kq_conv_b": kq_conv_b[l],
        "m_norm_g": m_norm_g[l], "hy_conv_w": hy_conv_w[l], "hy_conv_b": hy_conv_b[l],
        "hy_w1": hy_w1[l], "hy_b1": hy_b1[l], "hy_w2": hy_w2[l], "hy_b2": hy_b2[l],
        "hy_w3": hy_w3[l], "hy_b3": hy_b3[l], "hy_w4": hy_w4[l], "hy_freq": hy_freq[l],
        "hy_decay": hy_decay[l], "hy_skip": hy_skip[l],
        "w_pm": w_pm[l], "w_ph": w_ph[l], "w_out": w_out[l],
    }
    mod = (silu_c @ w_ada[l] + b_ada[l])[:, None, :]
    sh1, sc1, g1, sh2, sc2, g2 = jnp.split(mod, 6, axis=-1)
    modc = silu_cc @ w_ada[l] + b_ada[l]
    csh1, csc1, cg1, csh2, csc2, cg2 = jnp.split(modc, 6, axis=-1)
    hc = rmsnorm(ctx, norm1_g[l]) * (1.0 + csc1) + csh1
    khc, vhc, lgc = project_kvg(hc, lp, seq_conv)
    st_f, st_b = context_states(khc, vhc, lgc)
    h = rmsnorm(x, norm1_g[l]) * (1.0 + sc1) + sh1
    kh, vh, lg = project_kvg(h, lp, lat_conv)
    y = mixer(h, kh, vh, lg, st_f, st_b, lp, lat_conv, hyena_spectra(seq_len, lp))
    x = x + g1 * y
    h2 = rmsnorm(x, norm2_g[l]) * (1.0 + sc2) + sh2
    x = x + g2 * swiglu(h2, ffn_w1[l], ffn_w3[l], ffn_w2[l])
    return _final_norm(x, final_g)
```

```python
import functools
import math

import jax
import jax.numpy as jnp
from jax import lax
from jax.experimental import pallas as pl
from jax.experimental.pallas import tpu as pltpu

F32 = jnp.float32
BF16 = jnp.bfloat16

D_MODEL = 1024
M_HEADS = 4
HEAD_DIM = D_MODEL // M_HEADS
GRID_W = 64
SCAN_CHUNK = 256
HY_ORDER = 2
HY_EMB = 33
HY_BANDS = (HY_EMB - 1) // 2
HY_FREQ_BLOCKS = 4
D_FF = ((8 * D_MODEL + 3 * 256 - 1) // (3 * 256)) * 256
EPS = 1e-6
NEG = -1e30
LANES = 128
VMEM_LIMIT = 56 * 1024 * 1024

G_OFF = 2 * D_MODEL
Q_OFF = G_OFF + 4 * M_HEADS
O_OFF = Q_OFF + D_MODEL
HY_OFF = O_OFF + D_MODEL
MG_OFF = HY_OFF + 3 * D_MODEL

J_K, J_Q, J_HV, J_HX1, J_HX2, J_V, J_O, J_GM, J_GH = range(9)
LAT_TYPES = ("cs", "cs", "c", "c", "c", "id", "sg", "sg", "sg")
CTX_TYPES = ("cs", "id")


def _cparams(sem):
    return pltpu.CompilerParams(dimension_semantics=sem, vmem_limit_bytes=VMEM_LIMIT)


def _dot(a, b):
    return jnp.dot(a, b, preferred_element_type=F32)


def _sigmoid(x):
    return 1.0 / (1.0 + jnp.exp(-x))


def _norm_mod(x, g, sc, sh):
    y = x * lax.rsqrt(jnp.mean(x * x, axis=-1, keepdims=True) + EPS) * g
    return y * (1.0 + sc) + sh


def _mod_kernel(c_ref, w_ref, b_ref, o_ref):
    c = c_ref[...]
    s = (c * _sigmoid(c)).astype(BF16)
    o_ref[...] = _dot(s, w_ref[...].astype(BF16)) + b_ref[...]


def _mod(cs, w_ada, b_ada):
    rows, d = cs.shape
    n = w_ada.shape[1]
    tn = 1024
    return pl.pallas_call(
        _mod_kernel,
        grid=(n // tn,),
        in_specs=[pl.BlockSpec((rows, d), lambda j: (0, 0)),
                  pl.BlockSpec((d, tn), lambda j: (0, j)),
                  pl.BlockSpec((1, tn), lambda j: (0, j))],
        out_specs=pl.BlockSpec((rows, tn), lambda j: (0, j)),
        out_shape=jax.ShapeDtypeStruct((rows, n), F32),
        compiler_params=_cparams(("arbitrary",)),
        name="mod",
    )(cs, w_ada, b_ada)


def _inproj_kernel(types, period, x_ref, mod_ref, g_ref, w_ref, b_ref, cw_ref, o_ref, h_ref, acc_ref):
    j = pl.program_id(1)
    tm = x_ref.shape[0]

    @pl.when(j == 0)
    def _():
        h = _norm_mod(x_ref[...], g_ref[...], mod_ref[:, D_MODEL:2 * D_MODEL], mod_ref[:, 0:D_MODEL])
        h_ref[...] = h.astype(BF16)

    acc_ref[...] = _dot(h_ref[...], w_ref[...]) + b_ref[...]

    def conv():
        p = acc_ref[...]
        r = lax.broadcasted_iota(jnp.int32, (tm, 1), 0) % period
        prev = jnp.where(r == 0, 0.0, pltpu.roll(p, 1, axis=0))
        nxt = jnp.where(r == period - 1, 0.0, pltpu.roll(p, tm - 1, axis=0))
        return cw_ref[0:1, :] * prev + cw_ref[1:2, :] * p + cw_ref[2:3, :] * nxt + cw_ref[3:4, :]

    def emit(kind):
        if kind == "cs":
            y = conv()
            o_ref[...] = (y * _sigmoid(y) * cw_ref[4:5, :]).astype(o_ref.dtype)
        elif kind == "c":
            o_ref[...] = conv().astype(o_ref.dtype)
        elif kind == "id":
            o_ref[...] = acc_ref[...].astype(o_ref.dtype)
        else:
            o_ref[...] = _sigmoid(acc_ref[...]).astype(o_ref.dtype)

    for kind in sorted(set(types)):
        idx = [i for i, t in enumerate(types) if t == kind]
        lo, hi = idx[0], idx[-1]
        assert idx == list(range(lo, hi + 1))
        pl.when((j >= lo) & (j <= hi))(functools.partial(emit, kind))


def _inproj(x2, mod3, mod_row_fn, g, w_t, b_t, cw_t, types, period, tm):
    t, d = x2.shape
    nj = len(types)
    return pl.pallas_call(
        functools.partial(_inproj_kernel, types, period),
        grid=(t // tm, nj),
        in_specs=[pl.BlockSpec((tm, d), lambda i, j: (i, 0)),
                  pl.BlockSpec((None, 1, mod3.shape[2]), lambda i, j: (mod_row_fn(i), 0, 0)),
                  pl.BlockSpec((1, d), lambda i, j: (0, 0)),
                  pl.BlockSpec((None, d, d), lambda i, j: (j, 0, 0)),
                  pl.BlockSpec((None, 1, d), lambda i, j: (j, 0, 0)),
                  pl.BlockSpec((None, 8, d), lambda i, j: (j, 0, 0))],
        out_specs=pl.BlockSpec((None, tm, d), lambda i, j: (j, i, 0)),
        out_shape=jax.ShapeDtypeStruct((nj, t, d), BF16),
        scratch_shapes=[pltpu.VMEM((tm, d), BF16), pltpu.VMEM((tm, d), F32)],
        compiler_params=_cparams(("parallel", "arbitrary")),
        name="inproj",
    )(x2, mod3, g, w_t, b_t, cw_t)


def _gates_kernel(x_ref, mod_ref, g_ref, w_ref, b_ref, gc_ref, gr_ref):
    tm = x_ref.shape[0]
    h = _norm_mod(x_ref[...], g_ref[...], mod_ref[:, D_MODEL:2 * D_MODEL], mod_ref[:, 0:D_MODEL])
    pg = _dot(h.astype(BF16), w_ref[...]) + b_ref[...]
    lane = lax.broadcasted_iota(jnp.int32, (1, pg.shape[1]), 1) % LANES
    log_f = jnp.minimum(pg, 0.0) - jnp.log(1.0 + jnp.exp(-jnp.abs(pg)))
    pg = jnp.where((lane == 1) | (lane == 3), log_f, pg)
    tt = lax.broadcasted_iota(jnp.int32, (SCAN_CHUNK, SCAN_CHUNK), 0)
    ss = lax.broadcasted_iota(jnp.int32, (SCAN_CHUNK, SCAN_CHUNK), 1)
    ltri = (ss <= tt).astype(F32)
    utri = (ss >= tt).astype(F32)
    for c in range(tm // SCAN_CHUNK):
        blk = pg[c * SCAN_CHUNK:(c + 1) * SCAN_CHUNK]
        cf = jnp.dot(ltri, blk, preferred_element_type=F32, precision=lax.Precision.HIGHEST)
        cb = jnp.dot(utri, blk, preferred_element_type=F32, precision=lax.Precision.HIGHEST)
        res = jnp.where(lane == 1, cf, jnp.where(lane == 3, cb, blk))
        gc_ref[c * SCAN_CHUNK:(c + 1) * SCAN_CHUNK, :] = res
        gr_ref[:, c * SCAN_CHUNK:(c + 1) * SCAN_CHUNK] = res.T


def _gates(x2, mod3, mod_row_fn, g, wg, bg, tm):
    t, d = x2.shape
    n = wg.shape[1]
    return pl.pallas_call(
        _gates_kernel,
        grid=(t // tm,),
        in_specs=[pl.BlockSpec((tm, d), lambda i: (i, 0)),
                  pl.BlockSpec((None, 1, mod3.shape[2]), lambda i: (mod_row_fn(i), 0, 0)),
                  pl.BlockSpec((1, d), lambda i: (0, 0)),
                  pl.BlockSpec((d, n), lambda i: (0, 0)),
                  pl.BlockSpec((1, n), lambda i: (0, 0))],
        out_specs=[pl.BlockSpec((tm, n), lambda i: (i, 0)),
                   pl.BlockSpec((n, tm), lambda i: (0, i))],
        out_shape=[jax.ShapeDtypeStruct((t, n), F32), jax.ShapeDtypeStruct((n, t), F32)],
        compiler_params=_cparams(("parallel",)),
        name="gates",
    )(x2, mod3, g, wg, bg)


def _scan_step(q, k, v, icol, bcol, irow, brow, b_last, ct_ref, n_ref, m_ref, forward, want_out):
    t = k.shape[0]
    m = m_ref[...]
    r = irow - brow
    m_new = jnp.maximum(b_last + m, b_last + jnp.max(r, axis=1, keepdims=True))
    hout = None
    if want_out:
        tt = lax.broadcasted_iota(jnp.int32, (t, t), 0)
        ss = lax.broadcasted_iota(jnp.int32, (t, t), 1)
        mask = (ss <= tt) if forward else (ss >= tt)
        log_d = jnp.where(mask, bcol + r, NEG)
        inter = bcol + m
        m_t = jnp.maximum(inter, jnp.max(log_d, axis=1, keepdims=True))
        s = lax.dot_general(q, k, (((1,), (1,)), ((), ())), preferred_element_type=F32) * jnp.exp(log_d - m_t)
        a = jnp.exp(inter - m_t)
        num = _dot(s.astype(BF16), v) + a * _dot(q, ct_ref[...].astype(BF16))
        den = jnp.sum(s, axis=1, keepdims=True) + a * jnp.sum(q.astype(F32) * n_ref[...], axis=1, keepdims=True)
        hout = num / jnp.maximum(jnp.abs(den), jnp.exp(-m_t))
    w = jnp.exp(b_last - bcol + icol - m_new)
    decay = jnp.exp(b_last + m - m_new)
    wv = (w * v.astype(F32)).astype(BF16)
    ct_ref[...] = decay * ct_ref[...] + lax.dot_general(k, wv, (((0,), (0,)), ((), ())), preferred_element_type=F32)
    n_ref[...] = decay * n_ref[...] + jnp.sum(w * k.astype(F32), axis=0, keepdims=True)
    m_ref[...] = m_new
    return hout


def _mlstm_kernel(q_ref, k_ref, v_ref, o_ref, kc_ref, vc_ref, gc_ref, gr_ref, gcc_ref, grc_ref, g_ref,
                  out_ref, hf_ref, hb_ref, ctf_ref, ctb_ref, nf_ref, nb_ref, mf_ref, mb_ref):
    n_chunks = q_ref.shape[0] // SCAN_CHUNK
    t = SCAN_CHUNK
    for ct_ref, n_ref, m_ref in ((ctf_ref, nf_ref, mf_ref), (ctb_ref, nb_ref, mb_ref)):
        ct_ref[...] = jnp.zeros_like(ct_ref)
        n_ref[...] = jnp.zeros_like(n_ref)
        m_ref[...] = jnp.zeros_like(m_ref)

    kc = kc_ref[...]
    vc = vc_ref[...]
    _scan_step(None, kc, vc, gcc_ref[:, 0:1], gcc_ref[:, 1:2], grc_ref[0:1, :], grc_ref[1:2, :],
               gcc_ref[t - 1:t, 1:2], ctf_ref, nf_ref, mf_ref, True, False)
    _scan_step(None, kc, vc, gcc_ref[:, 2:3], gcc_ref[:, 3:4], grc_ref[2:3, :], grc_ref[3:4, :],
               gcc_ref[0:1, 3:4], ctb_ref, nb_ref, mb_ref, False, False)

    def body(c, carry):
        rf = pl.ds(pl.multiple_of(c * t, t), t)
        hf_ref[rf, :] = _scan_step(
            q_ref[rf, :], k_ref[rf, :], v_ref[rf, :], gc_ref[rf, 0:1], gc_ref[rf, 1:2],
            gr_ref[0:1, rf], gr_ref[1:2, rf], gc_ref[pl.ds(c * t + t - 1, 1), 1:2],
            ctf_ref, nf_ref, mf_ref, True, True)
        cb = n_chunks - 1 - c
        rb = pl.ds(pl.multiple_of(cb * t, t), t)
        hb_ref[rb, :] = _scan_step(
            q_ref[rb, :], k_ref[rb, :], v_ref[rb, :], gc_ref[rb, 2:3], gc_ref[rb, 3:4],
            gr_ref[2:3, rb], gr_ref[3:4, rb], gc_ref[pl.ds(cb * t, 1), 3:4],
            ctb_ref, nb_ref, mb_ref, False, True)
        return carry

    lax.fori_loop(0, n_chunks, body, 0)

    hs = hf_ref[...] + hb_ref[...]
    mu = jnp.mean(hs, axis=1, keepdims=True)
    dv = hs - mu
    var = jnp.mean(dv * dv, axis=1, keepdims=True)
    y = dv * lax.rsqrt(var + EPS) * g_ref[...] * o_ref[...].astype(F32)
    out_ref[...] = y.astype(out_ref.dtype)


def _mlstm(streams, ctx_streams, gc, gr, gcc, grc, m_norm_g, bsz, seq, ctx_len):
    t_all = bsz * seq
    hd = HEAD_DIM

    def stream(jidx):
        return pl.BlockSpec((None, seq, hd), lambda b, h: (jidx, b, h))

    def cstream(jidx):
        return pl.BlockSpec((None, ctx_len, hd), lambda b, h: (jidx, b, h))

    return pl.pallas_call(
        _mlstm_kernel,
        grid=(bsz, M_HEADS),
        in_specs=[stream(J_Q), stream(J_K), stream(J_V), stream(J_O), cstream(0), cstream(1),
                  pl.BlockSpec((seq, LANES), lambda b, h: (b, h)),
                  pl.BlockSpec((LANES, seq), lambda b, h: (h, b)),
                  pl.BlockSpec((ctx_len, LANES), lambda b, h: (b, h)),
                  pl.BlockSpec((LANES, ctx_len), lambda b, h: (h, b)),
                  pl.BlockSpec((1, hd), lambda b, h: (0, h))],
        out_specs=pl.BlockSpec((seq, hd), lambda b, h: (b, h)),
        out_shape=jax.ShapeDtypeStruct((t_all, D_MODEL), BF16),
        scratch_shapes=[pltpu.VMEM((seq, hd), F32), pltpu.VMEM((seq, hd), F32),
                        pltpu.VMEM((hd, hd), F32), pltpu.VMEM((hd, hd), F32),
                        pltpu.VMEM((1, hd), F32), pltpu.VMEM((1, hd), F32),
                        pltpu.VMEM((1, 1), F32), pltpu.VMEM((1, 1), F32)],
        compiler_params=_cparams(("parallel", "parallel")),
        name="mlstm",
    )(streams, streams, streams, streams, ctx_streams, ctx_streams, gc, gr, gcc, grc, m_norm_g)


def _filt_kernel(z_ref, w1_ref, b1_ref, w2_ref, b2_ref, w3_ref, b3_ref, fr_ref, w4_ref, dec_ref, e_ref, d_ref):
    hp = lax.Precision.HIGHEST
    n = z_ref.shape[0]
    c = e_ref.shape[1]
    fr = fr_ref[...]
    a = jnp.sin(fr * (jnp.dot(z_ref[...], w1_ref[...], preferred_element_type=F32, precision=hp) + b1_ref[...]))
    a = jnp.sin(fr * (jnp.dot(a, w2_ref[...], preferred_element_type=F32, precision=hp) + b2_ref[...]))
    a = jnp.sin(fr * (jnp.dot(a, w3_ref[...], preferred_element_type=F32, precision=hp) + b3_ref[...]))
    taps = jnp.dot(a, w4_ref[...], preferred_element_type=F32, precision=hp)
    tcol = z_ref[:, 0:1]
    kf = taps[:, :c] * jnp.exp(-tcol * jnp.abs(dec_ref[0:1, :]))
    kb = taps[:, c:] * jnp.exp(-tcol * jnp.abs(dec_ref[1:2, :]))
    row0 = lax.broadcasted_iota(jnp.int32, (n, 1), 0) == 0
    hplus = jnp.where(row0, kf + kb, kf)
    hminus = jnp.where(row0, 0.0, kb)
    e_ref[...] = hplus + hminus
    d_ref[...] = hplus - hminus


def _filters(z, w1, b1, w2, b2, w3, b3, fr, w4, dec):
    n = z.shape[0]
    c = D_MODEL
    fw = w1.shape[1]
    full = lambda shape: pl.BlockSpec(shape, lambda o: (0,) * len(shape))
    return pl.pallas_call(
        _filt_kernel,
        grid=(HY_ORDER,),
        in_specs=[full(z.shape), full(w1.shape), full(b1.shape), full(w2.shape), full(b2.shape),
                  full(w3.shape), full(b3.shape), full(fr.shape),
                  pl.BlockSpec((fw, 2 * c), lambda o: (0, o)),
                  pl.BlockSpec((None, 2, c), lambda o: (o, 0, 0))],
        out_specs=[pl.BlockSpec((None, n, c), lambda o: (o, 0, 0)),
                   pl.BlockSpec((None, n, c), lambda o: (o, 0, 0))],
        out_shape=[jax.ShapeDtypeStruct((HY_ORDER, n, c), F32)] * 2,
        compiler_params=_cparams(("arbitrary",)),
        name="filt",
    )(z, w1, b1, w2, b2, w3, b3, fr, w4, dec)


def _spec_kernel(cm_ref, sm_ref, sgn_ref, e_ref, d_ref, kre_ref, q_ref, nyq_ref):
    n = e_ref.shape[0]
    e = e_ref[...]
    krow = lax.broadcasted_iota(jnp.int32, (n, 1), 0)
    scale = jnp.where(krow == 0, 1.0, 2.0) / (2.0 * n)
    kre_ref[...] = _dot(cm_ref[...], e.astype(BF16)) * scale
    q_ref[...] = _dot(sm_ref[...], d_ref[...].astype(BF16)) * scale
    nyq_ref[...] = jnp.sum(e * sgn_ref[...], axis=0, keepdims=True) / (2.0 * n)


def _spectra(cm, sm, sgn, e, d, ct):
    n = e.shape[1]
    c = e.shape[2]
    const = lambda shape: pl.BlockSpec(shape, lambda o, j: (0,) * len(shape))
    tile = pl.BlockSpec((None, n, ct), lambda o, j: (o, 0, j))
    return pl.pallas_call(
        _spec_kernel,
        grid=(HY_ORDER, c // ct),
        in_specs=[const(cm.shape), const(sm.shape), const(sgn.shape), tile, tile],
        out_specs=[tile, tile, pl.BlockSpec((None, 1, ct), lambda o, j: (o, 0, j))],
        out_shape=[jax.ShapeDtypeStruct((HY_ORDER, n, c), F32), jax.ShapeDtypeStruct((HY_ORDER, n, c), F32),
                   jax.ShapeDtypeStruct((HY_ORDER, 1, c), F32)],
        compiler_params=_cparams(("arbitrary", "arbitrary")),
        name="spec",
    )(cm, sm, sgn, e, d)


def _hyena_kernel(cm_ref, sm_ref, sgn_ref, kre_ref, kq_ref, nyq_ref, skip_ref, v_ref, x1_ref, x2_ref, o_ref,
                  z_ref, zb_ref, pre_ref, pim_ref):
    n = cm_ref.shape[0]
    fb = n // HY_FREQ_BLOCKS
    sgn = sgn_ref[...]
    z_ref[...] = v_ref[...].astype(F32)
    gates = (x1_ref, x2_ref)
    for o in range(HY_ORDER):
        zb_ref[...] = z_ref[...].astype(BF16)
        for f in range(HY_FREQ_BLOCKS):
            rows = slice(f * fb, (f + 1) * fb)
            sre = _dot(cm_ref[rows, :], zb_ref[...])
            tim = _dot(sm_ref[rows, :], zb_ref[...])
            kre = kre_ref[o, rows, :]
            kq = kq_ref[o, rows, :]
            pre_ref[rows, :] = (sre * kre - tim * kq).astype(BF16)
            pim_ref[rows, :] = (sre * kq + tim * kre).astype(BF16)
        z = z_ref[...]
        znyq = jnp.sum(z * sgn, axis=0, keepdims=True)
        y = _dot(cm_ref[...], pre_ref[...]) + _dot(sm_ref[...], pim_ref[...]) + sgn * (znyq * nyq_ref[o])
        z_ref[...] = gates[o][...].astype(F32) * (y + skip_ref[o:o + 1, :] * z)
    o_ref[...] = z_ref[...].astype(o_ref.dtype)


def _hyena(cm, sm, sgn, kre, kq, nyq, skip, streams, bsz, seq, ct):
    c = D_MODEL
    const1 = lambda shape: pl.BlockSpec(shape, lambda j, b: (0,) * len(shape), pipeline_mode=pl.Buffered(1))
    ktile = pl.BlockSpec((HY_ORDER, seq, ct), lambda j, b: (0, 0, j), pipeline_mode=pl.Buffered(1))

    def stream(jidx):
        return pl.BlockSpec((None, seq, ct), lambda j, b: (jidx, b, j))

    return pl.pallas_call(
        _hyena_kernel,
        grid=(c // ct, bsz),
        in_specs=[const1(cm.shape), const1(sm.shape), const1(sgn.shape), ktile, ktile,
                  pl.BlockSpec((HY_ORDER, 1, ct), lambda j, b: (0, 0, j)),
                  pl.BlockSpec((HY_ORDER, ct), lambda j, b: (0, j)),
                  stream(J_HV), stream(J_HX1), stream(J_HX2)],
        out_specs=pl.BlockSpec((seq, ct), lambda j, b: (b, j)),
        out_shape=jax.ShapeDtypeStruct((bsz * seq, c), BF16),
        scratch_shapes=[pltpu.VMEM((seq, ct), F32), pltpu.VMEM((seq, ct), BF16),
                        pltpu.VMEM((seq, ct), BF16), pltpu.VMEM((seq, ct), BF16)],
        compiler_params=_cparams(("parallel", "parallel")),
        name="hyena",
    )(cm, sm, sgn, kre, kq, nyq, skip, streams, streams, streams)


def _tail_kernel(x_ref, mod_ref, hm_ref, hh_ref, gm_ref, gh_ref, wpm_ref, wph_ref, wout_ref,
                 n2_ref, w1_ref, w3_ref, w2_ref, fg_ref, o_ref):
    d = D_MODEL
    g1 = mod_ref[:, 2 * d:3 * d]
    sh2 = mod_ref[:, 3 * d:4 * d]
    sc2 = mod_ref[:, 4 * d:5 * d]
    g2 = mod_ref[:, 5 * d:6 * d]
    y = (gm_ref[...].astype(F32) * _dot(hm_ref[...], wpm_ref[...])
         + gh_ref[...].astype(F32) * _dot(hh_ref[...], wph_ref[...]))
    x1 = x_ref[...] + g1 * _dot(y.astype(BF16), wout_ref[...])
    h2 = _norm_mod(x1, n2_ref[...], sc2, sh2).astype(BF16)
    u = _dot(h2, w1_ref[...])
    a = (u * _sigmoid(u) * _dot(h2, w3_ref[...])).astype(BF16)
    x2 = x1 + g2 * _dot(a, w2_ref[...])
    o_ref[...] = x2 * lax.rsqrt(jnp.mean(x2 * x2, axis=-1, keepdims=True) + EPS) * fg_ref[...]


def _tail(x2d, mod3, hm, hh, streams, wpm, wph, wout, n2g, w1, w3, w2, fg, seq, tm):
    t, d = x2d.shape
    tiles_per_b = seq // tm
    const1 = lambda shape: pl.BlockSpec(shape, lambda i: (0,) * len(shape), pipeline_mode=pl.Buffered(1))
    rows = pl.BlockSpec((tm, d), lambda i: (i, 0))

    def stream(jidx):
        return pl.BlockSpec((None, tm, d), lambda i: (jidx, i, 0))

    return pl.pallas_call(
        _tail_kernel,
        grid=(t // tm,),
        in_specs=[rows, pl.BlockSpec((None, 1, mod3.shape[2]), lambda i: (i // tiles_per_b, 0, 0)),
                  rows, rows, stream(J_GM), stream(J_GH),
                  const1(wpm.shape), const1(wph.shape), const1(wout.shape), const1(n2g.shape),
                  const1(w1.shape), const1(w3.shape), const1(w2.shape), const1(fg.shape)],
        out_specs=rows,
        out_shape=jax.ShapeDtypeStruct((t, d), F32),
        compiler_params=_cparams(("parallel",)),
        name="tail",
    )(x2d, mod3, hm, hh, streams, streams, wpm, wph, wout, n2g, w1, w3, w2, fg)


def _dft_tables(n):
    k = lax.broadcasted_iota(jnp.int32, (n, n), 0)
    t = lax.broadcasted_iota(jnp.int32, (n, n), 1)
    ang = ((k * t) % (2 * n)).astype(F32) * (math.pi / n)
    sgn = 1.0 - 2.0 * (jnp.arange(n, dtype=jnp.int32) % 2).astype(F32)
    return jnp.cos(ang).astype(BF16), jnp.sin(ang).astype(BF16), sgn[:, None]


def _pos_embedding(n):
    t = jnp.linspace(0.0, 1.0, n, dtype=F32)[:, None]
    bands = jnp.linspace(1e-4, HY_BANDS - 1, HY_BANDS, dtype=F32)
    ang = (2.0 * math.pi / n) * jnp.arange(n, dtype=F32)[:, None] * bands[None, :]
    return jnp.concatenate([t, jnp.cos(ang), -jnp.sin(ang)], axis=-1)


def kernel(x, c, ctx, c_ctx, w_ada, b_ada, norm1_g, w_in, b_in, kq_conv_w, kq_conv_b, m_norm_g, hy_conv_w, hy_conv_b, hy_w1, hy_b1, hy_w2, hy_b2, hy_w3, hy_b3, hy_w4, hy_freq, hy_decay, hy_skip, w_pm, w_ph, w_out, norm2_g, ffn_w1, ffn_w3, ffn_w2, final_g):
    assert w_ada.shape[0] == 1, "single trunk layer"
    bsz, seq, d = x.shape
    ctx_len = ctx.shape[1]
    assert d == D_MODEL and seq % SCAN_CHUNK == 0 and ctx_len == SCAN_CHUNK
    row = lambda v: v.reshape(1, -1)

    wi, bi = w_in[0], b_in[0]
    col = lambda off: slice(off, off + d)
    order = [col(0), col(Q_OFF), col(HY_OFF), col(HY_OFF + d), col(HY_OFF + 2 * d),
             col(d), col(O_OFF), col(MG_OFF), col(MG_OFF + d)]
    w_t = jnp.stack([wi[:, s] for s in order]).astype(BF16)
    b_t = jnp.stack([bi[s] for s in order])[:, None, :]
    kqw, kqb, hyw, hyb = kq_conv_w[0], kq_conv_b[0], hy_conv_w[0], hy_conv_b[0]
    ones = jnp.ones((d,), F32)
    zeros = jnp.zeros((3, d), F32)

    def conv_rows(w3, cb, scale):
        return jnp.concatenate([w3, cb[None], (scale * ones)[None], zeros], axis=0)

    cw_t = jnp.stack([conv_rows(kqw[:, :d], kqb[:d], HEAD_DIM ** -0.5), conv_rows(kqw[:, d:], kqb[d:], 1.0)]
                     + [conv_rows(hyw[:, col(i * d)], hyb[col(i * d)], 1.0) for i in range(3)]
                     + [jnp.zeros((8, d), F32)] * 4)
    ctx_sel = jnp.array([J_K, J_V])
    gsrc = jnp.array([G_OFF + g * M_HEADS + h for h in range(M_HEADS) for g in range(4)])
    gdst = jnp.array([LANES * h + g for h in range(M_HEADS) for g in range(4)])
    wg = jnp.zeros((d, M_HEADS * LANES), F32).at[:, gdst].set(wi[:, gsrc]).astype(BF16)
    bg = jnp.zeros((1, M_HEADS * LANES), F32).at[0, gdst].set(bi[gsrc])

    n_mod = ((bsz + 1 + 7) // 8) * 8
    cs = jnp.zeros((n_mod, d), F32).at[:bsz].set(c).at[bsz].set(c_ctx)
    mod3 = _mod(cs, w_ada[0], row(b_ada[0]))[:, None, :]

    x2 = x.reshape(bsz * seq, d)
    ctx2 = ctx.reshape(bsz * ctx_len, d)
    g1n = row(norm1_g[0])
    tm = min(1024, seq)
    tmc = min(1024, bsz * ctx_len)
    lat_row = lambda i: i // (seq // tm)
    ctx_row = lambda i: bsz

    streams = _inproj(x2, mod3, lat_row, g1n, w_t, b_t, cw_t, LAT_TYPES, GRID_W, tm)
    ctx_streams = _inproj(ctx2, mod3, ctx_row, g1n, w_t[ctx_sel], b_t[ctx_sel], cw_t[ctx_sel], CTX_TYPES, ctx_len, tmc)
    gc, gr = _gates(x2, mod3, lat_row, g1n, wg, bg, tm)
    gcc, grc = _gates(ctx2, mod3, ctx_row, g1n, wg, bg, tmc)
    hm = _mlstm(streams, ctx_streams, gc, gr, gcc, grc, row(m_norm_g[0]), bsz, seq, ctx_len)

    zemb = jnp.pad(_pos_embedding(seq), ((0, 0), (0, 7)))
    w1p = jnp.pad(hy_w1[0], ((0, 7), (0, 0)))
    e, dd = _filters(zemb, w1p, row(hy_b1[0]), hy_w2[0], row(hy_b2[0]), hy_w3[0], row(hy_b3[0]),
                     row(hy_freq[0]), hy_w4[0], hy_decay[0])
    cm, sm, sgn = _dft_tables(seq)
    ct = 256
    kre, kq, nyq = _spectra(cm, sm, sgn, e, dd, ct)
    hh = _hyena(cm, sm, sgn, kre, kq, nyq, hy_skip[0], streams, bsz, seq, ct)

    out = _tail(x2, mod3, hm, hh, streams, w_pm[0].astype(BF16), w_ph[0].astype(BF16), w_out[0].astype(BF16),
                row(norm2_g[0]), ffn_w1[0].astype(BF16), ffn_w3[0].astype(BF16), ffn_w2[0].astype(BF16),
                row(final_g), seq, 512)
    return out.reshape(bsz, seq, d)
```

```python
import functools
import math

import jax
import jax.numpy as jnp
from jax import lax
from jax.experimental import pallas as pl
from jax.experimental.pallas import tpu as pltpu

F32 = jnp.float32
BF16 = jnp.bfloat16

D_MODEL = 1024
M_HEADS = 4
HEAD_DIM = D_MODEL // M_HEADS
GRID_W = 64
SCAN_CHUNK = 256
HY_ORDER = 2
HY_EMB = 33
HY_BANDS = (HY_EMB - 1) // 2
HY_FREQ_BLOCKS = 4
D_FF = ((8 * D_MODEL + 3 * 256 - 1) // (3 * 256)) * 256
EPS = 1e-6
NEG = -1e30
LANES = 128
SUBLANES = 8
VMEM_LIMIT = 56 * 1024 * 1024

G_OFF = 2 * D_MODEL
Q_OFF = G_OFF + 4 * M_HEADS
O_OFF = Q_OFF + D_MODEL
HY_OFF = O_OFF + D_MODEL
MG_OFF = HY_OFF + 3 * D_MODEL

J_K, J_Q, J_HV, J_HX1, J_HX2, J_V, J_O, J_GM, J_GH = range(9)
N_STREAMS = 9
N_CONV_TILES = 2
CONV_COL_BLOCKS = 4
GATE_ROWS = ("i_fwd", "cum_f_fwd", "i_bwd", "revcum_f_bwd")


def _cparams(sem):
    return pltpu.CompilerParams(dimension_semantics=sem, vmem_limit_bytes=VMEM_LIMIT)


def _dot(a, b):
    return jnp.dot(a, b, preferred_element_type=F32)


def _sigmoid(x):
    return 1.0 / (1.0 + jnp.exp(-x))


def _norm_mod(x, g, sc, sh):
    y = x * lax.rsqrt(jnp.mean(x * x, axis=-1, keepdims=True) + EPS) * g
    return y * (1.0 + sc) + sh


def _token_conv(p, taps, period):
    n, c = p.shape
    g = p.reshape(n // period, period, c)
    zero = jnp.zeros((n // period, 1, c), p.dtype)
    prev = jnp.concatenate([zero, g[:, :-1, :]], axis=1)
    nxt = jnp.concatenate([g[:, 1:, :], zero], axis=1)
    y = taps[0:1] * prev + taps[1:2] * g + taps[2:3] * nxt + taps[3:4]
    return y.reshape(n, c)


def _mod_kernel(c_ref, w_ref, b_ref, o_ref):
    c = c_ref[...]
    s = (c * _sigmoid(c)).astype(BF16)
    o_ref[...] = _dot(s, w_ref[...].astype(BF16)) + b_ref[...]


def _mod(cs, w_ada, b_ada):
    rows, d = cs.shape
    n = w_ada.shape[1]
    tn = 1024
    return pl.pallas_call(
        _mod_kernel,
        grid=(n // tn,),
        in_specs=[pl.BlockSpec((rows, d), lambda j: (0, 0)),
                  pl.BlockSpec((d, tn), lambda j: (0, j)),
                  pl.BlockSpec((1, tn), lambda j: (0, j))],
        out_specs=pl.BlockSpec((rows, tn), lambda j: (0, j)),
        out_shape=jax.ShapeDtypeStruct((rows, n), F32),
        compiler_params=_cparams(("arbitrary",)),
        name="mod",
    )(cs, w_ada, b_ada)


def _chunk_gates(pg):
    t = pg.shape[0]
    lane = lax.broadcasted_iota(jnp.int32, (1, pg.shape[1]), 1) % SUBLANES
    tt = lax.broadcasted_iota(jnp.int32, (t, t), 0)
    ss = lax.broadcasted_iota(jnp.int32, (t, t), 1)
    ltri = jnp.where(ss <= tt, 1.0, 0.0).astype(BF16)
    hi = pg.astype(BF16)
    r1 = pg - hi.astype(F32)
    mid = r1.astype(BF16)
    lo = (r1 - mid.astype(F32)).astype(BF16)
    cf = _dot(ltri, hi) + _dot(ltri, mid) + _dot(ltri, lo)
    rev = cf[t - 1:t, :] - cf + pg
    return jnp.where(lane == 1, cf, jnp.where(lane == 3, rev, pg))


def _inproj_kernel(n_conv, period, x_ref, mod_ref, g_ref, w_ref, b_ref, cw_ref, wg_ref, bg_ref,
                   o_ref, gr_ref, h_ref):
    j = pl.program_id(1)
    tm = x_ref.shape[0]

    @pl.when(j == 0)
    def _():
        h = _norm_mod(x_ref[...], g_ref[...], mod_ref[:, D_MODEL:2 * D_MODEL], mod_ref[:, 0:D_MODEL]).astype(BF16)
        h_ref[...] = h
        pg = _dot(h, wg_ref[...]) + bg_ref[...]
        lane = lax.broadcasted_iota(jnp.int32, (1, pg.shape[1]), 1) % SUBLANES
        log_f = jnp.minimum(pg, 0.0) - jnp.log(1.0 + jnp.exp(-jnp.abs(pg)))
        pg = jnp.where((lane == 1) | (lane == 3), log_f, pg)
        for c in range(tm // SCAN_CHUNK):
            cols = slice(c * SCAN_CHUNK, (c + 1) * SCAN_CHUNK)
            res_t = _chunk_gates(pg[cols]).T
            for hd in range(M_HEADS):
                gr_ref[hd, :, cols] = res_t[hd * SUBLANES:(hd + 1) * SUBLANES, :]

    @pl.when(j < n_conv)
    def _():
        nb = D_MODEL // CONV_COL_BLOCKS

        def mm(n):
            cols = slice(n * nb, (n + 1) * nb)
            return _dot(h_ref[...], w_ref[:, cols]) + b_ref[:, cols]

        acc = mm(0)
        for n in range(CONV_COL_BLOCKS):
            nxt = mm(n + 1) if n + 1 < CONV_COL_BLOCKS else None
            cols = slice(n * nb, (n + 1) * nb)
            y = _token_conv(acc, cw_ref[:, cols], period)
            o_ref[:, cols] = (y * _sigmoid(y)).astype(o_ref.dtype)
            acc = nxt

    @pl.when(j >= n_conv)
    def _():
        o_ref[...] = (_dot(h_ref[...], w_ref[...]) + b_ref[...]).astype(o_ref.dtype)


def _inproj(x2, mod3, mod_row_fn, g, w_t, b_t, cw_t, wg, bg, n_conv, period, tm):
    t, d = x2.shape
    nj = w_t.shape[0]
    return pl.pallas_call(
        functools.partial(_inproj_kernel, n_conv, period),
        grid=(t // tm, nj),
        in_specs=[pl.BlockSpec((tm, d), lambda i, j: (i, 0)),
                  pl.BlockSpec((None, 1, mod3.shape[2]), lambda i, j: (mod_row_fn(i), 0, 0)),
                  pl.BlockSpec((1, d), lambda i, j: (0, 0)),
                  pl.BlockSpec((None, d, d), lambda i, j: (j, 0, 0)),
                  pl.BlockSpec((None, 1, d), lambda i, j: (j, 0, 0)),
                  pl.BlockSpec((None, SUBLANES, d), lambda i, j: (jnp.minimum(j, n_conv - 1), 0, 0)),
                  pl.BlockSpec((d, LANES), lambda i, j: (0, 0)),
                  pl.BlockSpec((1, LANES), lambda i, j: (0, 0))],
        out_specs=[pl.BlockSpec((None, tm, d), lambda i, j: (j, i, 0)),
                   pl.BlockSpec((M_HEADS, SUBLANES, tm), lambda i, j: (0, 0, i))],
        out_shape=[jax.ShapeDtypeStruct((nj, t, d), BF16),
                   jax.ShapeDtypeStruct((M_HEADS, SUBLANES, t), F32)],
        scratch_shapes=[pltpu.VMEM((tm, d), BF16)],
        compiler_params=_cparams(("parallel", "arbitrary")),
        name="inproj",
    )(x2, mod3, g, w_t, b_t, cw_t, wg, bg)


def _scan_step(q, k, v, gates, ctn_ref, m_ref, forward, want_out):
    t = k.shape[0]
    row0 = 0 if forward else 2
    i_row = gates[row0:row0 + 1, :]
    b_row = gates[row0 + 1:row0 + 2, :]
    b_last = b_row[:, t - 1:t] if forward else b_row[:, 0:1]
    m = m_ref[...]
    r = i_row - b_row
    m_new = b_last + jnp.maximum(m, jnp.max(r, axis=1, keepdims=True))
    k_t = k.T
    v1 = jnp.concatenate([v, jnp.ones((t, LANES), v.dtype)], axis=1)
    hout = None
    if want_out:
        tt = lax.broadcasted_iota(jnp.int32, (t, t), 0)
        ss = lax.broadcasted_iota(jnp.int32, (t, t), 1)
        allowed = (ss <= tt) if forward else (ss >= tt)
        m1 = jnp.where(allowed, r, NEG)
        cm = jnp.maximum(m, jnp.max(m1, axis=1, keepdims=True))
        b_col = jnp.sum(jnp.where(ss == tt, b_row, 0.0), axis=1, keepdims=True)
        s = _dot(q, k_t) * jnp.exp(m1 - cm)
        a = jnp.exp(m - cm)
        tot = _dot(s.astype(BF16), v1) + a * _dot(q, ctn_ref[...].astype(BF16))
        den = tot[:, HEAD_DIM:]
        rden = 1.0 / jnp.maximum(jnp.abs(den), jnp.exp(-(b_col + cm)) * (HEAD_DIM ** 0.5))
        hout = tot[:, :HEAD_DIM] * jnp.concatenate([rden] * (HEAD_DIM // LANES), axis=1)
    w_row = jnp.exp(b_last - m_new + r)
    decay = jnp.exp(b_last + m - m_new)
    kw = (k_t.astype(F32) * w_row).astype(BF16)
    ctn_ref[...] = decay * ctn_ref[...] + _dot(kw, v1)
    m_ref[...] = m_new
    return hout


def _mlstm_kernel(q_ref, k_ref, v_ref, o_ref, kc_ref, vc_ref, gr_ref, grc_ref, g_ref,
                  out_ref, hf_ref, hb_ref, ctf_ref, ctb_ref, mf_ref, mb_ref):
    n_chunks = q_ref.shape[0] // SCAN_CHUNK
    t = SCAN_CHUNK
    for ct_ref, m_ref in ((ctf_ref, mf_ref), (ctb_ref, mb_ref)):
        ct_ref[...] = jnp.zeros_like(ct_ref)
        m_ref[...] = jnp.zeros_like(m_ref)

    kc = kc_ref[...]
    vc = vc_ref[...]
    gc = grc_ref[...]
    _scan_step(None, kc, vc, gc, ctf_ref, mf_ref, True, False)
    _scan_step(None, kc, vc, gc, ctb_ref, mb_ref, False, False)

    def body(c, carry):
        rf = pl.ds(pl.multiple_of(c * t, t), t)
        hf_ref[rf, :] = _scan_step(q_ref[rf, :], k_ref[rf, :], v_ref[rf, :], gr_ref[:, rf],
                                   ctf_ref, mf_ref, True, True)
        rb = pl.ds(pl.multiple_of((n_chunks - 1 - c) * t, t), t)
        hb_ref[rb, :] = _scan_step(q_ref[rb, :], k_ref[rb, :], v_ref[rb, :], gr_ref[:, rb],
                                   ctb_ref, mb_ref, False, True)
        return carry

    lax.fori_loop(0, n_chunks, body, 0)

    hs = hf_ref[...] + hb_ref[...]
    mu = jnp.mean(hs, axis=1, keepdims=True)
    dv = hs - mu
    var = jnp.mean(dv * dv, axis=1, keepdims=True)
    y = dv * lax.rsqrt(var + EPS) * g_ref[...] * _sigmoid(o_ref[...].astype(F32))
    out_ref[...] = y.astype(out_ref.dtype)


def _mlstm(streams, ctx_streams, gr, grc, m_norm_g, bsz, seq, ctx_len):
    hd = HEAD_DIM

    def stream(jidx):
        return pl.BlockSpec((None, seq, hd), lambda b, h: (jidx, b, h))

    def cstream(jidx):
        return pl.BlockSpec((None, ctx_len, hd), lambda b, h: (jidx, b, h))

    return pl.pallas_call(
        _mlstm_kernel,
        grid=(bsz, M_HEADS),
        in_specs=[stream(J_Q), stream(J_K), stream(J_V), stream(J_O), cstream(0), cstream(1),
                  pl.BlockSpec((None, SUBLANES, seq), lambda b, h: (h, 0, b)),
                  pl.BlockSpec((None, SUBLANES, ctx_len), lambda b, h: (h, 0, b)),
                  pl.BlockSpec((1, hd), lambda b, h: (0, h))],
        out_specs=pl.BlockSpec((seq, hd), lambda b, h: (b, h)),
        out_shape=jax.ShapeDtypeStruct((bsz * seq, D_MODEL), BF16),
        scratch_shapes=[pltpu.VMEM((seq, hd), F32), pltpu.VMEM((seq, hd), F32),
                        pltpu.VMEM((hd, hd + LANES), F32), pltpu.VMEM((hd, hd + LANES), F32),
                        pltpu.VMEM((1, 1), F32), pltpu.VMEM((1, 1), F32)],
        compiler_params=_cparams(("parallel", "parallel")),
        name="mlstm",
    )(streams, streams, streams, streams, ctx_streams, ctx_streams, gr, grc, m_norm_g)


def _filt_kernel(z_ref, w1_ref, b1_ref, w2_ref, b2_ref, w3_ref, b3_ref, fr_ref, w4_ref, dec_ref, e_ref, d_ref):
    hp = lax.Precision.HIGHEST
    n = z_ref.shape[0]
    c = e_ref.shape[1]
    fr = fr_ref[...]
    a = jnp.sin(fr * (jnp.dot(z_ref[...], w1_ref[...], preferred_element_type=F32, precision=hp) + b1_ref[...]))
    a = jnp.sin(fr * (jnp.dot(a, w2_ref[...], preferred_element_type=F32, precision=hp) + b2_ref[...]))
    a = jnp.sin(fr * (jnp.dot(a, w3_ref[...], preferred_element_type=F32, precision=hp) + b3_ref[...]))
    taps = jnp.dot(a, w4_ref[...], preferred_element_type=F32, precision=hp)
    tcol = z_ref[:, 0:1]
    kf = taps[:, :c] * jnp.exp(-tcol * jnp.abs(dec_ref[0:1, :]))
    kb = taps[:, c:] * jnp.exp(-tcol * jnp.abs(dec_ref[1:2, :]))
    row0 = lax.broadcasted_iota(jnp.int32, (n, 1), 0) == 0
    hplus = jnp.where(row0, kf + kb, kf)
    hminus = jnp.where(row0, 0.0, kb)
    e_ref[...] = hplus + hminus
    d_ref[...] = hplus - hminus


def _filters(z, w1, b1, w2, b2, w3, b3, fr, w4, dec):
    n = z.shape[0]
    c = D_MODEL
    fw = w1.shape[1]
    full = lambda shape: pl.BlockSpec(shape, lambda o: (0,) * len(shape))
    return pl.pallas_call(
        _filt_kernel,
        grid=(HY_ORDER,),
        in_specs=[full(z.shape), full(w1.shape), full(b1.shape), full(w2.shape), full(b2.shape),
                  full(w3.shape), full(b3.shape), full(fr.shape),
                  pl.BlockSpec((fw, 2 * c), lambda o: (0, o)),
                  pl.BlockSpec((None, 2, c), lambda o: (o, 0, 0))],
        out_specs=[pl.BlockSpec((None, n, c), lambda o: (o, 0, 0)),
                   pl.BlockSpec((None, n, c), lambda o: (o, 0, 0))],
        out_shape=[jax.ShapeDtypeStruct((HY_ORDER, n, c), F32)] * 2,
        compiler_params=_cparams(("arbitrary",)),
        name="filt",
    )(z, w1, b1, w2, b2, w3, b3, fr, w4, dec)


def _spec_kernel(cm_ref, sm_ref, sgn_ref, e_ref, d_ref, kre_ref, q_ref, nyq_ref):
    n = e_ref.shape[0]
    e = e_ref[...]
    krow = lax.broadcasted_iota(jnp.int32, (n, 1), 0)
    scale = jnp.where(krow == 0, 1.0, 2.0) / (2.0 * n)
    kre_ref[...] = _dot(cm_ref[...], e.astype(BF16)) * scale
    q_ref[...] = _dot(sm_ref[...], d_ref[...].astype(BF16)) * scale
    nyq_ref[...] = jnp.sum(e * sgn_ref[...], axis=0, keepdims=True) / (2.0 * n)


def _spectra(cm, sm, sgn, e, d, ct):
    n = e.shape[1]
    c = e.shape[2]
    const = lambda shape: pl.BlockSpec(shape, lambda o, j: (0,) * len(shape))
    tile = pl.BlockSpec((None, n, ct), lambda o, j: (o, 0, j))
    return pl.pallas_call(
        _spec_kernel,
        grid=(HY_ORDER, c // ct),
        in_specs=[const(cm.shape), const(sm.shape), const(sgn.shape), tile, tile],
        out_specs=[tile, tile, pl.BlockSpec((None, 1, ct), lambda o, j: (o, 0, j))],
        out_shape=[jax.ShapeDtypeStruct((HY_ORDER, n, c), F32), jax.ShapeDtypeStruct((HY_ORDER, n, c), F32),
                   jax.ShapeDtypeStruct((HY_ORDER, 1, c), F32)],
        compiler_params=_cparams(("arbitrary", "arbitrary")),
        name="spec",
    )(cm, sm, sgn, e, d)


def _hyena_kernel(cm_ref, sm_ref, sgn_ref, kre_ref, kq_ref, nyq_ref, skip_ref, cw_ref, v_ref, x1_ref, x2_ref, o_ref,
                  z_ref, zb_ref, pre_ref, pim_ref):
    n = cm_ref.shape[0]
    fb = n // HY_FREQ_BLOCKS
    sgn = sgn_ref[...]
    z_ref[...] = _token_conv(v_ref[...].astype(F32), cw_ref[0], GRID_W)
    gates = (x1_ref, x2_ref)
    for o in range(HY_ORDER):
        zb_ref[...] = z_ref[...].astype(BF16)
        for f in range(HY_FREQ_BLOCKS):
            rows = slice(f * fb, (f + 1) * fb)
            sre = _dot(cm_ref[rows, :], zb_ref[...])
            tim = _dot(sm_ref[rows, :], zb_ref[...])
            kre = kre_ref[o, rows, :]
            kq = kq_ref[o, rows, :]
            pre_ref[rows, :] = (sre * kre - tim * kq).astype(BF16)
            pim_ref[rows, :] = (sre * kq + tim * kre).astype(BF16)
        z = z_ref[...]
        znyq = jnp.sum(z * sgn, axis=0, keepdims=True)
        y = _dot(cm_ref[...], pre_ref[...]) + _dot(sm_ref[...], pim_ref[...]) + sgn * (znyq * nyq_ref[o])
        gate = _token_conv(gates[o][...].astype(F32), cw_ref[o + 1], GRID_W)
        z_ref[...] = gate * (y + skip_ref[o:o + 1, :] * z)
    o_ref[...] = z_ref[...].astype(o_ref.dtype)


def _hyena(cm, sm, sgn, kre, kq, nyq, skip, cw_h, streams, bsz, seq, ct):
    c = D_MODEL
    const1 = lambda shape: pl.BlockSpec(shape, lambda j, b: (0,) * len(shape), pipeline_mode=pl.Buffered(1))
    ktile = pl.BlockSpec((HY_ORDER, seq, ct), lambda j, b: (0, 0, j), pipeline_mode=pl.Buffered(1))

    def stream(jidx):
        return pl.BlockSpec((None, seq, ct), lambda j, b: (jidx, b, j))

    return pl.pallas_call(
        _hyena_kernel,
        grid=(c // ct, bsz),
        in_specs=[const1(cm.shape), const1(sm.shape), const1(sgn.shape), ktile, ktile,
                  pl.BlockSpec((HY_ORDER, 1, ct), lambda j, b: (0, 0, j)),
                  pl.BlockSpec((HY_ORDER, ct), lambda j, b: (0, j)),
                  pl.BlockSpec((HY_ORDER + 1, SUBLANES, ct), lambda j, b: (0, 0, j)),
                  stream(J_HV), stream(J_HX1), stream(J_HX2)],
        out_specs=pl.BlockSpec((seq, ct), lambda j, b: (b, j)),
        out_shape=jax.ShapeDtypeStruct((bsz * seq, c), BF16),
        scratch_shapes=[pltpu.VMEM((seq, ct), F32), pltpu.VMEM((seq, ct), BF16),
                        pltpu.VMEM((seq, ct), BF16), pltpu.VMEM((seq, ct), BF16)],
        compiler_params=_cparams(("parallel", "parallel")),
        name="hyena",
    )(cm, sm, sgn, kre, kq, nyq, skip, cw_h, streams, streams, streams)


def _tail_kernel(x_ref, mod_ref, hm_ref, hh_ref, gm_ref, gh_ref, wpm_ref, wph_ref, wout_ref,
                 n2_ref, w1_ref, w3_ref, w2_ref, fg_ref, o_ref):
    d = D_MODEL
    g1 = mod_ref[:, 2 * d:3 * d]
    sh2 = mod_ref[:, 3 * d:4 * d]
    sc2 = mod_ref[:, 4 * d:5 * d]
    g2 = mod_ref[:, 5 * d:6 * d]
    y = (_sigmoid(gm_ref[...].astype(F32)) * _dot(hm_ref[...], wpm_ref[...])
         + _sigmoid(gh_ref[...].astype(F32)) * _dot(hh_ref[...], wph_ref[...]))
    x1 = x_ref[...] + g1 * _dot(y.astype(BF16), wout_ref[...])
    h2 = _norm_mod(x1, n2_ref[...], sc2, sh2).astype(BF16)
    u = _dot(h2, w1_ref[...])
    a = (u * _sigmoid(u) * _dot(h2, w3_ref[...])).astype(BF16)
    x2 = x1 + g2 * _dot(a, w2_ref[...])
    o_ref[...] = x2 * lax.rsqrt(jnp.mean(x2 * x2, axis=-1, keepdims=True) + EPS) * fg_ref[...]


def _tail(x2d, mod3, hm, hh, streams, wpm, wph, wout, n2g, w1, w3, w2, fg, seq, tm):
    t, d = x2d.shape
    tiles_per_b = seq // tm
    const1 = lambda shape: pl.BlockSpec(shape, lambda i: (0,) * len(shape), pipeline_mode=pl.Buffered(1))
    rows = pl.BlockSpec((tm, d), lambda i: (i, 0))

    def stream(jidx):
        return pl.BlockSpec((None, tm, d), lambda i: (jidx, i, 0))

    return pl.pallas_call(
        _tail_kernel,
        grid=(t // tm,),
        in_specs=[rows, pl.BlockSpec((None, 1, mod3.shape[2]), lambda i: (i // tiles_per_b, 0, 0)),
                  rows, rows, stream(J_GM), stream(J_GH),
                  const1(wpm.shape), const1(wph.shape), const1(wout.shape), const1(n2g.shape),
                  const1(w1.shape), const1(w3.shape), const1(w2.shape), const1(fg.shape)],
        out_specs=rows,
        out_shape=jax.ShapeDtypeStruct((t, d), F32),
        compiler_params=_cparams(("parallel",)),
        name="tail",
    )(x2d, mod3, hm, hh, streams, streams, wpm, wph, wout, n2g, w1, w3, w2, fg)


def _dft_tables(n):
    k = lax.broadcasted_iota(jnp.int32, (n, n), 0)
    t = lax.broadcasted_iota(jnp.int32, (n, n), 1)
    ang = ((k * t) % (2 * n)).astype(F32) * (math.pi / n)
    sgn = 1.0 - 2.0 * (jnp.arange(n, dtype=jnp.int32) % 2).astype(F32)
    return jnp.cos(ang).astype(BF16), jnp.sin(ang).astype(BF16), sgn[:, None]


def _pos_embedding(n):
    t = jnp.linspace(0.0, 1.0, n, dtype=F32)[:, None]
    bands = jnp.linspace(1e-4, HY_BANDS - 1, HY_BANDS, dtype=F32)
    ang = (2.0 * math.pi / n) * jnp.arange(n, dtype=F32)[:, None] * bands[None, :]
    return jnp.concatenate([t, jnp.cos(ang), -jnp.sin(ang)], axis=-1)


def kernel(x, c, ctx, c_ctx, w_ada, b_ada, norm1_g, w_in, b_in, kq_conv_w, kq_conv_b, m_norm_g, hy_conv_w, hy_conv_b, hy_w1, hy_b1, hy_w2, hy_b2, hy_w3, hy_b3, hy_w4, hy_freq, hy_decay, hy_skip, w_pm, w_ph, w_out, norm2_g, ffn_w1, ffn_w3, ffn_w2, final_g):
    assert w_ada.shape[0] == 1, "single trunk layer"
    bsz, seq, d = x.shape
    ctx_len = ctx.shape[1]
    assert d == D_MODEL and seq % SCAN_CHUNK == 0 and ctx_len == SCAN_CHUNK
    row = lambda v: v.reshape(1, -1)

    wi, bi = w_in[0], b_in[0]
    col = lambda off: slice(off, off + d)
    order = [col(0), col(Q_OFF), col(HY_OFF), col(HY_OFF + d), col(HY_OFF + 2 * d),
             col(d), col(O_OFF), col(MG_OFF), col(MG_OFF + d)]
    w_t = jnp.stack([wi[:, s] for s in order]).astype(BF16)
    b_t = jnp.stack([bi[s] for s in order])[:, None, :]
    kqw, kqb, hyw, hyb = kq_conv_w[0], kq_conv_b[0], hy_conv_w[0], hy_conv_b[0]
    ones = jnp.ones((d,), F32)
    zeros = jnp.zeros((SUBLANES - 5, d), F32)

    def conv_rows(w3, cb, scale):
        return jnp.concatenate([w3, cb[None], (scale * ones)[None], zeros], axis=0)

    cw_kq = jnp.stack([conv_rows(kqw[:, :d], kqb[:d], 1.0), conv_rows(kqw[:, d:], kqb[d:], 1.0)])
    cw_h = jnp.stack([conv_rows(hyw[:, col(i * d)], hyb[col(i * d)], 1.0) for i in range(3)])
    ctx_sel = jnp.array([J_K, J_V])
    gsrc = jnp.array([G_OFF + g * M_HEADS + h for h in range(M_HEADS) for g in range(4)])
    gdst = jnp.array([SUBLANES * h + g for h in range(M_HEADS) for g in range(4)])
    wg = jnp.zeros((d, LANES), F32).at[:, gdst].set(wi[:, gsrc]).astype(BF16)
    bg = jnp.zeros((1, LANES), F32).at[0, gdst].set(bi[gsrc])

    n_mod = ((bsz + 1 + 7) // 8) * 8
    cs = jnp.zeros((n_mod, d), F32).at[:bsz].set(c).at[bsz].set(c_ctx)
    mod3 = _mod(cs, w_ada[0], row(b_ada[0]))[:, None, :]

    x2 = x.reshape(bsz * seq, d)
    ctx2 = ctx.reshape(bsz * ctx_len, d)
    g1n = row(norm1_g[0])
    tm = min(1024, seq)
    tmc = min(1024, bsz * ctx_len)
    lat_row = lambda i: i // (seq // tm)
    ctx_row = lambda i: bsz

    streams, gr = _inproj(x2, mod3, lat_row, g1n, w_t, b_t, cw_kq, wg, bg, N_CONV_TILES, GRID_W, tm)
    ctx_streams, grc = _inproj(ctx2, mod3, ctx_row, g1n, w_t[ctx_sel], b_t[ctx_sel], cw_kq[:1], wg, bg,
                               1, ctx_len, tmc)
    hm = _mlstm(streams, ctx_streams, gr, grc, row(m_norm_g[0]), bsz, seq, ctx_len)

    zemb = jnp.pad(_pos_embedding(seq), ((0, 0), (0, 7)))
    w1p = jnp.pad(hy_w1[0], ((0, 7), (0, 0)))
    e, dd = _filters(zemb, w1p, row(hy_b1[0]), hy_w2[0], row(hy_b2[0]), hy_w3[0], row(hy_b3[0]),
                     row(hy_freq[0]), hy_w4[0], hy_decay[0])
    cm, sm, sgn = _dft_tables(seq)
    ct = 256
    kre, kq, nyq = _spectra(cm, sm, sgn, e, dd, ct)
    hh = _hyena(cm, sm, sgn, kre, kq, nyq, hy_skip[0], cw_h, streams, bsz, seq, ct)

    out = _tail(x2, mod3, hm, hh, streams, w_pm[0].astype(BF16), w_ph[0].astype(BF16), w_out[0].astype(BF16),
                row(norm2_g[0]), ffn_w1[0].astype(BF16), ffn_w3[0].astype(BF16), ffn_w2[0].astype(BF16),
                row(final_g), seq, 512)
    return out.reshape(bsz, seq, d)
```

```python
import functools
import math

import jax
import jax.numpy as jnp
from jax import lax
from jax.experimental import pallas as pl
from jax.experimental.pallas import tpu as pltpu

F32 = jnp.float32
BF16 = jnp.bfloat16

D_MODEL = 1024
M_HEADS = 4
HEAD_DIM = D_MODEL // M_HEADS
GRID_W = 64
SCAN_CHUNK = 256
HY_ORDER = 2
HY_EMB = 33
HY_BANDS = (HY_EMB - 1) // 2
INPROJ_ROWS = 512
TAIL_ROWS = 512
HY_CHANNELS = 256
D_FF = ((8 * D_MODEL + 3 * 256 - 1) // (3 * 256)) * 256
EPS = 1e-6
NEG = -1e30
LANES = 128
SUBLANES = 8
VMEM_LIMIT = 56 * 1024 * 1024

G_OFF = 2 * D_MODEL
Q_OFF = G_OFF + 4 * M_HEADS
O_OFF = Q_OFF + D_MODEL
HY_OFF = O_OFF + D_MODEL
MG_OFF = HY_OFF + 3 * D_MODEL

J_K, J_Q, J_HV, J_HX1, J_HX2, J_V, J_O, J_GM, J_GH = range(9)
N_STREAMS = 9
N_CONV_TILES = 2
CONV_COL_BLOCKS = 4
GATE_ROWS = ("i_fwd", "cum_f_fwd", "i_bwd", "revcum_f_bwd")


def _cparams(sem):
    return pltpu.CompilerParams(dimension_semantics=sem, vmem_limit_bytes=VMEM_LIMIT)


def _dot(a, b):
    return jnp.dot(a, b, preferred_element_type=F32)


def _sigmoid(x):
    return 1.0 / (1.0 + jnp.exp(-x))


def _norm_mod(x, g, sc, sh):
    y = x * lax.rsqrt(jnp.mean(x * x, axis=-1, keepdims=True) + EPS) * g
    return y * (1.0 + sc) + sh


def _token_conv(p, taps, period):
    n, c = p.shape
    g = p.reshape(n // period, period, c)
    zero = jnp.zeros((n // period, 1, c), p.dtype)
    prev = jnp.concatenate([zero, g[:, :-1, :]], axis=1)
    nxt = jnp.concatenate([g[:, 1:, :], zero], axis=1)
    y = taps[0:1] * prev + taps[1:2] * g + taps[2:3] * nxt + taps[3:4]
    return y.reshape(n, c)


def _mod_kernel(c_ref, w_ref, b_ref, o_ref):
    c = c_ref[...]
    s = (c * _sigmoid(c)).astype(BF16)
    o_ref[...] = _dot(s, w_ref[...].astype(BF16)) + b_ref[...]


def _mod(cs, w_ada, b_ada):
    rows, d = cs.shape
    n = w_ada.shape[1]
    tn = 1024
    return pl.pallas_call(
        _mod_kernel,
        grid=(n // tn,),
        in_specs=[pl.BlockSpec((rows, d), lambda j: (0, 0)),
                  pl.BlockSpec((d, tn), lambda j: (0, j)),
                  pl.BlockSpec((1, tn), lambda j: (0, j))],
        out_specs=pl.BlockSpec((rows, tn), lambda j: (0, j)),
        out_shape=jax.ShapeDtypeStruct((rows, n), F32),
        compiler_params=_cparams(("arbitrary",)),
        name="mod",
    )(cs, w_ada, b_ada)


def _chunk_gates(pg):
    t = pg.shape[0]
    lane = lax.broadcasted_iota(jnp.int32, (1, pg.shape[1]), 1) % SUBLANES
    tt = lax.broadcasted_iota(jnp.int32, (t, t), 0)
    ss = lax.broadcasted_iota(jnp.int32, (t, t), 1)
    ltri = jnp.where(ss <= tt, 1.0, 0.0).astype(BF16)
    hi = pg.astype(BF16)
    r1 = pg - hi.astype(F32)
    mid = r1.astype(BF16)
    lo = (r1 - mid.astype(F32)).astype(BF16)
    cf = _dot(ltri, hi) + _dot(ltri, mid) + _dot(ltri, lo)
    rev = cf[t - 1:t, :] - cf + pg
    return jnp.where(lane == 1, cf, jnp.where(lane == 3, rev, pg))


def _inproj_kernel(n_conv, period, x_ref, mod_ref, g_ref, w_ref, b_ref, cw_ref, wg_ref, bg_ref,
                   o_ref, gr_ref, h_ref):
    tm = x_ref.shape[0]
    nj = w_ref.shape[0]
    h = _norm_mod(x_ref[...], g_ref[...], mod_ref[:, D_MODEL:2 * D_MODEL], mod_ref[:, 0:D_MODEL]).astype(BF16)
    h_ref[...] = h
    pg = _dot(h, wg_ref[...]) + bg_ref[...]
    lane = lax.broadcasted_iota(jnp.int32, (1, pg.shape[1]), 1) % SUBLANES
    log_f = jnp.minimum(pg, 0.0) - jnp.log(1.0 + jnp.exp(-jnp.abs(pg)))
    pg = jnp.where((lane == 1) | (lane == 3), log_f, pg)
    for c in range(tm // SCAN_CHUNK):
        cols = slice(c * SCAN_CHUNK, (c + 1) * SCAN_CHUNK)
        res_t = _chunk_gates(pg[cols]).T
        for hd in range(M_HEADS):
            gr_ref[hd, :, cols] = res_t[hd * SUBLANES:(hd + 1) * SUBLANES, :]

    nb = D_MODEL // CONV_COL_BLOCKS
    pending = []
    for j in range(n_conv):
        for n in range(CONV_COL_BLOCKS):
            cols = slice(n * nb, (n + 1) * nb)
            pending.append((j, cols, _dot(h_ref[...], w_ref[j, :, cols]) + b_ref[j, :, cols]))

    def finish(count):
        for _ in range(min(count, len(pending))):
            j, cols, acc = pending.pop(0)
            y = _token_conv(acc, cw_ref[j, :, cols], period)
            o_ref[j, :, cols] = (y * _sigmoid(y)).astype(o_ref.dtype)

    per_tile = -(-len(pending) // max(nj - n_conv, 1))
    for j in range(n_conv, nj):
        o_ref[j] = (_dot(h_ref[...], w_ref[j]) + b_ref[j]).astype(o_ref.dtype)
        finish(per_tile)
    finish(len(pending))


def _inproj(x2, mod3, mod_row_fn, g, w_t, b_t, cw_t, wg, bg, n_conv, period, tm):
    t, d = x2.shape
    nj = w_t.shape[0]
    const1 = lambda shape: pl.BlockSpec(shape, lambda i: (0,) * len(shape), pipeline_mode=pl.Buffered(1))
    return pl.pallas_call(
        functools.partial(_inproj_kernel, n_conv, period),
        grid=(t // tm,),
        in_specs=[pl.BlockSpec((tm, d), lambda i: (i, 0)),
                  pl.BlockSpec((None, 1, mod3.shape[2]), lambda i: (mod_row_fn(i), 0, 0)),
                  const1((1, d)), const1(w_t.shape), const1(b_t.shape), const1(cw_t.shape),
                  const1((d, LANES)), const1((1, LANES))],
        out_specs=[pl.BlockSpec((nj, tm, d), lambda i: (0, i, 0)),
                   pl.BlockSpec((M_HEADS, SUBLANES, tm), lambda i: (0, 0, i))],
        out_shape=[jax.ShapeDtypeStruct((nj, t, d), BF16),
                   jax.ShapeDtypeStruct((M_HEADS, SUBLANES, t), F32)],
        scratch_shapes=[pltpu.VMEM((tm, d), BF16)],
        compiler_params=_cparams(("parallel",)),
        name="inproj",
    )(x2, mod3, g, w_t, b_t, cw_t, wg, bg)


def _scan_step(q, k, v, gates, ctn_ref, m_ref, forward, want_out):
    t = k.shape[0]
    row0 = 0 if forward else 2
    i_row = gates[row0:row0 + 1, :]
    b_row = gates[row0 + 1:row0 + 2, :]
    b_last = b_row[:, t - 1:t] if forward else b_row[:, 0:1]
    m = m_ref[...]
    r = i_row - b_row
    m_new = b_last + jnp.maximum(m, jnp.max(r, axis=1, keepdims=True))
    k_t = k.T
    v1 = jnp.concatenate([v, jnp.ones((t, LANES), v.dtype)], axis=1)
    hout = None
    if want_out:
        tt = lax.broadcasted_iota(jnp.int32, (t, t), 0)
        ss = lax.broadcasted_iota(jnp.int32, (t, t), 1)
        allowed = (ss <= tt) if forward else (ss >= tt)
        m1 = jnp.where(allowed, r, NEG)
        cm = jnp.maximum(m, jnp.max(m1, axis=1, keepdims=True))
        b_col = jnp.sum(jnp.where(ss == tt, b_row, 0.0), axis=1, keepdims=True)
        s = _dot(q, k_t) * jnp.exp(m1 - cm)
        a = jnp.exp(m - cm)
        tot = _dot(s.astype(BF16), v1) + a * _dot(q, ctn_ref[...].astype(BF16))
        den = tot[:, HEAD_DIM:]
        rden = 1.0 / jnp.maximum(jnp.abs(den), jnp.exp(-(b_col + cm)) * (HEAD_DIM ** 0.5))
        hout = tot[:, :HEAD_DIM] * jnp.concatenate([rden] * (HEAD_DIM // LANES), axis=1)
    w_row = jnp.exp(b_last - m_new + r)
    decay = jnp.exp(b_last + m - m_new)
    kw = (k_t.astype(F32) * w_row).astype(BF16)
    ctn_ref[...] = decay * ctn_ref[...] + _dot(kw, v1)
    m_ref[...] = m_new
    return hout


def _mlstm_kernel(q_ref, k_ref, v_ref, o_ref, kc_ref, vc_ref, gr_ref, grc_ref, g_ref,
                  out_ref, hf_ref, hb_ref, ctf_ref, ctb_ref, mf_ref, mb_ref):
    n_chunks = q_ref.shape[0] // SCAN_CHUNK
    t = SCAN_CHUNK
    for ct_ref, m_ref in ((ctf_ref, mf_ref), (ctb_ref, mb_ref)):
        ct_ref[...] = jnp.zeros_like(ct_ref)
        m_ref[...] = jnp.zeros_like(m_ref)

    kc = kc_ref[...]
    vc = vc_ref[...]
    gc = grc_ref[...]
    _scan_step(None, kc, vc, gc, ctf_ref, mf_ref, True, False)
    _scan_step(None, kc, vc, gc, ctb_ref, mb_ref, False, False)

    def body(c, carry):
        rf = pl.ds(pl.multiple_of(c * t, t), t)
        hf_ref[rf, :] = _scan_step(q_ref[rf, :], k_ref[rf, :], v_ref[rf, :], gr_ref[:, rf],
                                   ctf_ref, mf_ref, True, True)
        rb = pl.ds(pl.multiple_of((n_chunks - 1 - c) * t, t), t)
        hb_ref[rb, :] = _scan_step(q_ref[rb, :], k_ref[rb, :], v_ref[rb, :], gr_ref[:, rb],
                                   ctb_ref, mb_ref, False, True)
        return carry

    lax.fori_loop(0, n_chunks, body, 0)

    hs = hf_ref[...] + hb_ref[...]
    mu = jnp.mean(hs, axis=1, keepdims=True)
    dv = hs - mu
    var = jnp.mean(dv * dv, axis=1, keepdims=True)
    y = dv * lax.rsqrt(var + EPS) * g_ref[...] * _sigmoid(o_ref[...].astype(F32))
    out_ref[...] = y.astype(out_ref.dtype)


def _mlstm(streams, ctx_streams, gr, grc, m_norm_g, bsz, seq, ctx_len):
    hd = HEAD_DIM

    def stream(jidx):
        return pl.BlockSpec((None, seq, hd), lambda b, h: (jidx, b, h))

    def cstream(jidx):
        return pl.BlockSpec((None, ctx_len, hd), lambda b, h: (jidx, b, h))

    return pl.pallas_call(
        _mlstm_kernel,
        grid=(bsz, M_HEADS),
        in_specs=[stream(J_Q), stream(J_K), stream(J_V), stream(J_O), cstream(0), cstream(1),
                  pl.BlockSpec((None, SUBLANES, seq), lambda b, h: (h, 0, b)),
                  pl.BlockSpec((None, SUBLANES, ctx_len), lambda b, h: (h, 0, b)),
                  pl.BlockSpec((1, hd), lambda b, h: (0, h))],
        out_specs=pl.BlockSpec((seq, hd), lambda b, h: (b, h)),
        out_shape=jax.ShapeDtypeStruct((bsz * seq, D_MODEL), BF16),
        scratch_shapes=[pltpu.VMEM((seq, hd), F32), pltpu.VMEM((seq, hd), F32),
                        pltpu.VMEM((hd, hd + LANES), F32), pltpu.VMEM((hd, hd + LANES), F32),
                        pltpu.VMEM((1, 1), F32), pltpu.VMEM((1, 1), F32)],
        compiler_params=_cparams(("parallel", "parallel")),
        name="mlstm",
    )(streams, streams, streams, streams, ctx_streams, ctx_streams, gr, grc, m_norm_g)


def _filt_kernel(z_ref, w1_ref, b1_ref, w2_ref, b2_ref, w3_ref, b3_ref, fr_ref, w4_ref, dec_ref, e_ref, d_ref):
    hp = lax.Precision.HIGHEST
    n = z_ref.shape[0]
    c = e_ref.shape[1]
    fr = fr_ref[...]
    a = jnp.sin(fr * (jnp.dot(z_ref[...], w1_ref[...], preferred_element_type=F32, precision=hp) + b1_ref[...]))
    a = jnp.sin(fr * (jnp.dot(a, w2_ref[...], preferred_element_type=F32, precision=hp) + b2_ref[...]))
    a = jnp.sin(fr * (jnp.dot(a, w3_ref[...], preferred_element_type=F32, precision=hp) + b3_ref[...]))
    taps = jnp.dot(a, w4_ref[...], preferred_element_type=F32, precision=hp)
    tcol = z_ref[:, 0:1]
    kf = taps[:, :c] * jnp.exp(-tcol * jnp.abs(dec_ref[0:1, :]))
    kb = taps[:, c:] * jnp.exp(-tcol * jnp.abs(dec_ref[1:2, :]))
    row0 = lax.broadcasted_iota(jnp.int32, (n, 1), 0) == 0
    hplus = jnp.where(row0, kf + kb, kf)
    hminus = jnp.where(row0, 0.0, kb)
    e_ref[...] = hplus + hminus
    d_ref[...] = hplus - hminus


def _filters(z, w1, b1, w2, b2, w3, b3, fr, w4, dec):
    n = z.shape[0]
    c = D_MODEL
    fw = w1.shape[1]
    full = lambda shape: pl.BlockSpec(shape, lambda o: (0,) * len(shape))
    return pl.pallas_call(
        _filt_kernel,
        grid=(HY_ORDER,),
        in_specs=[full(z.shape), full(w1.shape), full(b1.shape), full(w2.shape), full(b2.shape),
                  full(w3.shape), full(b3.shape), full(fr.shape),
                  pl.BlockSpec((fw, 2 * c), lambda o: (0, o)),
                  pl.BlockSpec((None, 2, c), lambda o: (o, 0, 0))],
        out_specs=[pl.BlockSpec((None, n, c), lambda o: (o, 0, 0)),
                   pl.BlockSpec((None, n, c), lambda o: (o, 0, 0))],
        out_shape=[jax.ShapeDtypeStruct((HY_ORDER, n, c), F32)] * 2,
        compiler_params=_cparams(("arbitrary",)),
        name="filt",
    )(z, w1, b1, w2, b2, w3, b3, fr, w4, dec)


FFT_R = 16
FFT_HALF = FFT_R // 2


def _axpy(acc, coef, x):
    coef = round(coef, 15)
    if abs(coef) < 1e-12:
        return acc
    if acc is None:
        return x if coef == 1.0 else (-x if coef == -1.0 else coef * x)
    if coef == 1.0:
        return acc + x
    if coef == -1.0:
        return acc - x
    return acc + coef * x


def _stage_a_fwd(k1, zs, ssum, sdif):
    even = k1 % 2 == 0
    re = _axpy(zs[0], math.cos(math.pi * k1 / 2), zs[FFT_HALF // 2])
    im = _axpy(None, -math.sin(math.pi * k1 / 2), zs[FFT_HALF // 2])
    for n in range(1, FFT_HALF // 2):
        th = 2 * math.pi * n * k1 / FFT_R
        re = _axpy(re, math.cos(th), ssum[n] if even else sdif[n])
        im = _axpy(im, -math.sin(th), sdif[n] if even else ssum[n])
    return re, im


class _StageAInverse:
    def __init__(self):
        self.y0 = None
        self.y4 = None
        self.t1 = [None] * (FFT_HALF // 2)
        self.t2 = [None] * (FFT_HALF // 2)

    def add(self, k1, vre, vim):
        even = k1 % 2 == 0
        self.y0 = _axpy(self.y0, 1.0, vre)
        self.y4 = _axpy(self.y4, math.cos(math.pi * k1 / 2), vre)
        if vim is not None:
            self.y4 = _axpy(self.y4, -math.sin(math.pi * k1 / 2), vim)
        for n in range(1, FFT_HALF // 2):
            th = 2 * math.pi * n * k1 / FFT_R
            if even:
                self.t1[n] = _axpy(self.t1[n], math.cos(th), vre)
                if vim is not None:
                    self.t2[n] = _axpy(self.t2[n], -math.sin(th), vim)
            else:
                self.t2[n] = _axpy(self.t2[n], math.cos(th), vre)
                if vim is not None:
                    self.t1[n] = _axpy(self.t1[n], -math.sin(th), vim)

    def slabs(self):
        ys = [None] * FFT_HALF
        ys[0], ys[FFT_HALF // 2] = self.y0, self.y4
        for n in range(1, FFT_HALF // 2):
            ys[n] = self.t1[n] + self.t2[n]
            ys[FFT_HALF - n] = self.t1[n] - self.t2[n]
        return ys


def _slabs(ref, m):
    zs = [ref[i * m:(i + 1) * m, :] for i in range(FFT_HALF)]
    ssum = [None] + [zs[n] + zs[FFT_HALF - n] for n in range(1, FFT_HALF // 2)]
    sdif = [None] + [zs[n] - zs[FFT_HALF - n] for n in range(1, FFT_HALF // 2)]
    return zs, ssum, sdif


def _stage_b_fwd(gb_ref, k1, re, im):
    m = re.shape[0]
    if im is None:
        return _dot(gb_ref[k1, :, 0:m], re.astype(BF16))
    return _dot(gb_ref[k1], jnp.concatenate([re, im], axis=0).astype(BF16))


def _spec_kernel(gb_ref, e_ref, d_ref, kre_ref, kim_ref):
    m = e_ref.shape[0] // FFT_HALF
    es = _slabs(e_ref, m)
    ds = _slabs(d_ref, m)
    for k1 in range(FFT_HALF + 1):
        w = (1.0 if k1 in (0, FFT_HALF) else 2.0) / (FFT_R * m)
        kre_ref[k1] = _stage_b_fwd(gb_ref, k1, *_stage_a_fwd(k1, *es))[0:m] * w
        kim_ref[k1] = _stage_b_fwd(gb_ref, k1, *_stage_a_fwd(k1, *ds))[m:2 * m] * w


def _spectra(gb, e, d, ct):
    n = e.shape[1]
    c = e.shape[2]
    m = n // FFT_HALF
    tile = pl.BlockSpec((None, n, ct), lambda o, j: (o, 0, j))
    ktile = pl.BlockSpec((None, FFT_HALF + 1, m, ct), lambda o, j: (o, 0, 0, j))
    kshape = jax.ShapeDtypeStruct((HY_ORDER, FFT_HALF + 1, m, c), F32)
    return pl.pallas_call(
        _spec_kernel,
        grid=(HY_ORDER, c // ct),
        in_specs=[pl.BlockSpec(gb.shape, lambda o, j: (0, 0, 0)), tile, tile],
        out_specs=[ktile, ktile],
        out_shape=[kshape, kshape],
        compiler_params=_cparams(("arbitrary", "arbitrary")),
        name="spec",
    )(gb, e, d)


def _hyena_kernel(gb_ref, hb_ref, kre_ref, kim_ref, skip_ref, cw_ref, v_ref, x1_ref, x2_ref, o_ref, z_ref):
    n = z_ref.shape[0]
    m = n // FFT_HALF
    z_ref[...] = _token_conv(v_ref[...].astype(F32), cw_ref[0], GRID_W)
    gates = (x1_ref, x2_ref)
    for o in range(HY_ORDER):
        zs, ssum, sdif = _slabs(z_ref, m)
        inv = _StageAInverse()
        for k1 in range(FFT_HALF + 1):
            re, im = _stage_a_fwd(k1, zs, ssum, sdif)
            x = _stage_b_fwd(gb_ref, k1, re, im)
            xre, xim = x[0:m], x[m:2 * m]
            kre, kim = kre_ref[o, k1], kim_ref[o, k1]
            p = jnp.concatenate([xre * kre - xim * kim, xre * kim + xim * kre], axis=0).astype(BF16)
            if im is None:
                inv.add(k1, _dot(hb_ref[k1, 0:m, :], p), None)
            else:
                v = _dot(hb_ref[k1], p)
                inv.add(k1, v[0:m], v[m:2 * m])
        ys = inv.slabs()
        gate = _token_conv(gates[o][...].astype(F32), cw_ref[o + 1], GRID_W)
        for i in range(FFT_HALF):
            rows = slice(i * m, (i + 1) * m)
            z_ref[rows, :] = gate[rows] * (ys[i] + skip_ref[o:o + 1, :] * zs[i])
    o_ref[...] = z_ref[...].astype(o_ref.dtype)


def _hyena(gb, hb, kre, kim, skip, cw_h, streams, bsz, seq, ct):
    c = D_MODEL
    m = seq // FFT_HALF
    const1 = lambda shape: pl.BlockSpec(shape, lambda j, b: (0,) * len(shape), pipeline_mode=pl.Buffered(1))
    ktile = pl.BlockSpec((HY_ORDER, FFT_HALF + 1, m, ct), lambda j, b: (0, 0, 0, j), pipeline_mode=pl.Buffered(1))

    def stream(jidx):
        return pl.BlockSpec((None, seq, ct), lambda j, b: (jidx, b, j))

    return pl.pallas_call(
        _hyena_kernel,
        grid=(c // ct, bsz),
        in_specs=[const1(gb.shape), const1(hb.shape), ktile, ktile,
                  pl.BlockSpec((HY_ORDER, ct), lambda j, b: (0, j)),
                  pl.BlockSpec((HY_ORDER + 1, SUBLANES, ct), lambda j, b: (0, 0, j)),
                  stream(J_HV), stream(J_HX1), stream(J_HX2)],
        out_specs=pl.BlockSpec((seq, ct), lambda j, b: (b, j)),
        out_shape=jax.ShapeDtypeStruct((bsz * seq, c), BF16),
        scratch_shapes=[pltpu.VMEM((seq, ct), F32)],
        compiler_params=_cparams(("parallel", "parallel")),
        name="hyena",
    )(gb, hb, kre, kim, skip, cw_h, streams, streams, streams)


def _tail_kernel(x_ref, mod_ref, hm_ref, hh_ref, gm_ref, gh_ref, wpm_ref, wph_ref, wout_ref,
                 n2_ref, w1_ref, w3_ref, w2_ref, fg_ref, o_ref):
    d = D_MODEL
    g1 = mod_ref[:, 2 * d:3 * d]
    sh2 = mod_ref[:, 3 * d:4 * d]
    sc2 = mod_ref[:, 4 * d:5 * d]
    g2 = mod_ref[:, 5 * d:6 * d]
    y = (_sigmoid(gm_ref[...].astype(F32)) * _dot(hm_ref[...], wpm_ref[...])
         + _sigmoid(gh_ref[...].astype(F32)) * _dot(hh_ref[...], wph_ref[...]))
    x1 = x_ref[...] + g1 * _dot(y.astype(BF16), wout_ref[...])
    h2 = _norm_mod(x1, n2_ref[...], sc2, sh2).astype(BF16)
    u = _dot(h2, w1_ref[...])
    a = (u * _sigmoid(u) * _dot(h2, w3_ref[...])).astype(BF16)
    x2 = x1 + g2 * _dot(a, w2_ref[...])
    o_ref[...] = x2 * lax.rsqrt(jnp.mean(x2 * x2, axis=-1, keepdims=True) + EPS) * fg_ref[...]


def _tail(x2d, mod3, hm, hh, streams, wpm, wph, wout, n2g, w1, w3, w2, fg, seq, tm):
    t, d = x2d.shape
    tiles_per_b = seq // tm
    const1 = lambda shape: pl.BlockSpec(shape, lambda i: (0,) * len(shape), pipeline_mode=pl.Buffered(1))
    rows = pl.BlockSpec((tm, d), lambda i: (i, 0))

    def stream(jidx):
        return pl.BlockSpec((None, tm, d), lambda i: (jidx, i, 0))

    return pl.pallas_call(
        _tail_kernel,
        grid=(t // tm,),
        in_specs=[rows, pl.BlockSpec((None, 1, mod3.shape[2]), lambda i: (i // tiles_per_b, 0, 0)),
                  rows, rows, stream(J_GM), stream(J_GH),
                  const1(wpm.shape), const1(wph.shape), const1(wout.shape), const1(n2g.shape),
                  const1(w1.shape), const1(w3.shape), const1(w2.shape), const1(fg.shape)],
        out_specs=rows,
        out_shape=jax.ShapeDtypeStruct((t, d), F32),
        compiler_params=_cparams(("parallel",)),
        name="tail",
    )(x2d, mod3, hm, hh, streams, streams, wpm, wph, wout, n2g, w1, w3, w2, fg)


def _stage_b_tables(n):
    m = n // FFT_HALF
    k1 = lax.broadcasted_iota(jnp.int32, (FFT_HALF + 1, m, m), 0)
    k2 = lax.broadcasted_iota(jnp.int32, (FFT_HALF + 1, m, m), 1)
    n2 = lax.broadcasted_iota(jnp.int32, (FFT_HALF + 1, m, m), 2)
    ang = ((n2 * (k1 + FFT_R * k2)) % (2 * n)).astype(F32) * (math.pi / n)
    cs, sn = jnp.cos(ang), jnp.sin(ang)
    gb = jnp.concatenate([jnp.concatenate([cs, sn], axis=2), jnp.concatenate([-sn, cs], axis=2)], axis=1)
    return gb.astype(BF16), jnp.swapaxes(gb, 1, 2).astype(BF16)


def _pos_embedding(n):
    t = jnp.linspace(0.0, 1.0, n, dtype=F32)[:, None]
    bands = jnp.linspace(1e-4, HY_BANDS - 1, HY_BANDS, dtype=F32)
    ang = (2.0 * math.pi / n) * jnp.arange(n, dtype=F32)[:, None] * bands[None, :]
    return jnp.concatenate([t, jnp.cos(ang), -jnp.sin(ang)], axis=-1)


def kernel(x, c, ctx, c_ctx, w_ada, b_ada, norm1_g, w_in, b_in, kq_conv_w, kq_conv_b, m_norm_g, hy_conv_w, hy_conv_b, hy_w1, hy_b1, hy_w2, hy_b2, hy_w3, hy_b3, hy_w4, hy_freq, hy_decay, hy_skip, w_pm, w_ph, w_out, norm2_g, ffn_w1, ffn_w3, ffn_w2, final_g):
    assert w_ada.shape[0] == 1, "single trunk layer"
    bsz, seq, d = x.shape
    ctx_len = ctx.shape[1]
    assert d == D_MODEL and seq % SCAN_CHUNK == 0 and ctx_len == SCAN_CHUNK
    row = lambda v: v.reshape(1, -1)

    wi, bi = w_in[0], b_in[0]
    col = lambda off: slice(off, off + d)
    order = [col(0), col(Q_OFF), col(HY_OFF), col(HY_OFF + d), col(HY_OFF + 2 * d),
             col(d), col(O_OFF), col(MG_OFF), col(MG_OFF + d)]
    w_t = jnp.stack([wi[:, s] for s in order]).astype(BF16)
    b_t = jnp.stack([bi[s] for s in order])[:, None, :]
    kqw, kqb, hyw, hyb = kq_conv_w[0], kq_conv_b[0], hy_conv_w[0], hy_conv_b[0]
    ones = jnp.ones((d,), F32)
    zeros = jnp.zeros((SUBLANES - 5, d), F32)

    def conv_rows(w3, cb, scale):
        return jnp.concatenate([w3, cb[None], (scale * ones)[None], zeros], axis=0)

    cw_kq = jnp.stack([conv_rows(kqw[:, :d], kqb[:d], 1.0), conv_rows(kqw[:, d:], kqb[d:], 1.0)])
    cw_h = jnp.stack([conv_rows(hyw[:, col(i * d)], hyb[col(i * d)], 1.0) for i in range(3)])
    ctx_sel = jnp.array([J_K, J_V])
    gsrc = jnp.array([G_OFF + g * M_HEADS + h for h in range(M_HEADS) for g in range(4)])
    gdst = jnp.array([SUBLANES * h + g for h in range(M_HEADS) for g in range(4)])
    wg = jnp.zeros((d, LANES), F32).at[:, gdst].set(wi[:, gsrc]).astype(BF16)
    bg = jnp.zeros((1, LANES), F32).at[0, gdst].set(bi[gsrc])

    n_mod = ((bsz + 1 + 7) // 8) * 8
    cs = jnp.zeros((n_mod, d), F32).at[:bsz].set(c).at[bsz].set(c_ctx)
    mod3 = _mod(cs, w_ada[0], row(b_ada[0]))[:, None, :]

    x2 = x.reshape(bsz * seq, d)
    ctx2 = ctx.reshape(bsz * ctx_len, d)
    g1n = row(norm1_g[0])
    tm = min(INPROJ_ROWS, seq)
    tmc = min(INPROJ_ROWS, bsz * ctx_len)
    lat_row = lambda i: i // (seq // tm)
    ctx_row = lambda i: bsz

    streams, gr = _inproj(x2, mod3, lat_row, g1n, w_t, b_t, cw_kq, wg, bg, N_CONV_TILES, GRID_W, tm)
    ctx_streams, grc = _inproj(ctx2, mod3, ctx_row, g1n, w_t[ctx_sel], b_t[ctx_sel], cw_kq[:1], wg, bg,
                               1, ctx_len, tmc)
    hm = _mlstm(streams, ctx_streams, gr, grc, row(m_norm_g[0]), bsz, seq, ctx_len)

    zemb = jnp.pad(_pos_embedding(seq), ((0, 0), (0, 7)))
    w1p = jnp.pad(hy_w1[0], ((0, 7), (0, 0)))
    e, dd = _filters(zemb, w1p, row(hy_b1[0]), hy_w2[0], row(hy_b2[0]), hy_w3[0], row(hy_b3[0]),
                     row(hy_freq[0]), hy_w4[0], hy_decay[0])
    gb, hb = _stage_b_tables(seq)
    kre, kim = _spectra(gb, e, dd, HY_CHANNELS)
    hh = _hyena(gb, hb, kre, kim, hy_skip[0], cw_h, streams, bsz, seq, HY_CHANNELS)

    out = _tail(x2, mod3, hm, hh, streams, w_pm[0].astype(BF16), w_ph[0].astype(BF16), w_out[0].astype(BF16),
                row(norm2_g[0]), ffn_w1[0].astype(BF16), ffn_w3[0].astype(BF16), ffn_w2[0].astype(BF16),
                row(final_g), seq, min(TAIL_ROWS, seq))
    return out.reshape(bsz, seq, d)
```

```python
import functools
import math

import jax
import jax.numpy as jnp
from jax import lax
from jax.experimental import pallas as pl
from jax.experimental.pallas import tpu as pltpu

F32 = jnp.float32
BF16 = jnp.bfloat16

D_MODEL = 1024
M_HEADS = 4
HEAD_DIM = D_MODEL // M_HEADS
GRID_W = 64
SCAN_CHUNK = 256
MLSTM_HEADS = 2
HY_ORDER = 2
HY_EMB = 33
HY_BANDS = (HY_EMB - 1) // 2
INPROJ_ROWS = 512
TAIL_ROWS = 512
HY_CHANNELS = 256
D_FF = ((8 * D_MODEL + 3 * 256 - 1) // (3 * 256)) * 256
EPS = 1e-6
NEG = -1e30
LANES = 128
SUBLANES = 8
VMEM_LIMIT = 56 * 1024 * 1024

G_OFF = 2 * D_MODEL
Q_OFF = G_OFF + 4 * M_HEADS
O_OFF = Q_OFF + D_MODEL
HY_OFF = O_OFF + D_MODEL
MG_OFF = HY_OFF + 3 * D_MODEL

J_K, J_Q, J_HV, J_HX1, J_HX2, J_V, J_O, J_GM, J_GH = range(9)
N_STREAMS = 9
N_CONV_TILES = 2
CONV_COL_BLOCKS = 4
GATE_ROWS = ("i_fwd", "cum_f_fwd", "i_bwd", "revcum_f_bwd")


def _cparams(sem):
    return pltpu.CompilerParams(dimension_semantics=sem, vmem_limit_bytes=VMEM_LIMIT)


def _dot(a, b):
    return jnp.dot(a, b, preferred_element_type=F32)


def _sigmoid(x):
    return 1.0 / (1.0 + jnp.exp(-x))


def _norm_mod(x, g, sc, sh):
    y = x * lax.rsqrt(jnp.mean(x * x, axis=-1, keepdims=True) + EPS) * g
    return y * (1.0 + sc) + sh


def _token_conv(p, taps, period):
    n, c = p.shape
    g = p.reshape(n // period, period, c)
    zero = jnp.zeros((n // period, 1, c), p.dtype)
    prev = jnp.concatenate([zero, g[:, :-1, :]], axis=1)
    nxt = jnp.concatenate([g[:, 1:, :], zero], axis=1)
    y = taps[0:1] * prev + taps[1:2] * g + taps[2:3] * nxt + taps[3:4]
    return y.reshape(n, c)


def _mod_kernel(c_ref, w_ref, b_ref, o_ref):
    c = c_ref[...]
    s = (c * _sigmoid(c)).astype(BF16)
    o_ref[...] = _dot(s, w_ref[...].astype(BF16)) + b_ref[...]


def _mod(cs, w_ada, b_ada):
    rows, d = cs.shape
    n = w_ada.shape[1]
    tn = 1024
    return pl.pallas_call(
        _mod_kernel,
        grid=(n // tn,),
        in_specs=[pl.BlockSpec((rows, d), lambda j: (0, 0)),
                  pl.BlockSpec((d, tn), lambda j: (0, j)),
                  pl.BlockSpec((1, tn), lambda j: (0, j))],
        out_specs=pl.BlockSpec((rows, tn), lambda j: (0, j)),
        out_shape=jax.ShapeDtypeStruct((rows, n), F32),
        compiler_params=_cparams(("arbitrary",)),
        name="mod",
    )(cs, w_ada, b_ada)


def _chunk_gates(pg):
    t = pg.shape[0]
    lane = lax.broadcasted_iota(jnp.int32, (1, pg.shape[1]), 1) % SUBLANES
    tt = lax.broadcasted_iota(jnp.int32, (t, t), 0)
    ss = lax.broadcasted_iota(jnp.int32, (t, t), 1)
    ltri = jnp.where(ss <= tt, 1.0, 0.0).astype(BF16)
    hi = pg.astype(BF16)
    r1 = pg - hi.astype(F32)
    mid = r1.astype(BF16)
    lo = (r1 - mid.astype(F32)).astype(BF16)
    cf = _dot(ltri, hi) + _dot(ltri, mid) + _dot(ltri, lo)
    rev = cf[t - 1:t, :] - cf + pg
    return jnp.where(lane == 1, cf, jnp.where(lane == 3, rev, pg))


def _inproj_kernel(n_conv, period, x_ref, mod_ref, g_ref, w_ref, b_ref, cw_ref, wg_ref, bg_ref,
                   o_ref, gr_ref, h_ref):
    tm = x_ref.shape[0]
    nj = w_ref.shape[0]
    h = _norm_mod(x_ref[...], g_ref[...], mod_ref[:, D_MODEL:2 * D_MODEL], mod_ref[:, 0:D_MODEL]).astype(BF16)
    h_ref[...] = h
    pg = _dot(h, wg_ref[...]) + bg_ref[...]
    lane = lax.broadcasted_iota(jnp.int32, (1, pg.shape[1]), 1) % SUBLANES
    log_f = jnp.minimum(pg, 0.0) - jnp.log(1.0 + jnp.exp(-jnp.abs(pg)))
    pg = jnp.where((lane == 1) | (lane == 3), log_f, pg)
    for c in range(tm // SCAN_CHUNK):
        cols = slice(c * SCAN_CHUNK, (c + 1) * SCAN_CHUNK)
        res_t = _chunk_gates(pg[cols]).T
        for hd in range(M_HEADS):
            gr_ref[hd, :, cols] = res_t[hd * SUBLANES:(hd + 1) * SUBLANES, :]

    nb = D_MODEL // CONV_COL_BLOCKS
    pending = []
    for j in range(n_conv):
        for n in range(CONV_COL_BLOCKS):
            cols = slice(n * nb, (n + 1) * nb)
            pending.append((j, cols, _dot(h_ref[...], w_ref[j, :, cols]) + b_ref[j, :, cols]))

    def finish(count):
        for _ in range(min(count, len(pending))):
            j, cols, acc = pending.pop(0)
            y = _token_conv(acc, cw_ref[j, :, cols], period)
            o_ref[j, :, cols] = (y * _sigmoid(y)).astype(o_ref.dtype)

    per_tile = -(-len(pending) // max(nj - n_conv, 1))
    for j in range(n_conv, nj):
        o_ref[j] = (_dot(h_ref[...], w_ref[j]) + b_ref[j]).astype(o_ref.dtype)
        finish(per_tile)
    finish(len(pending))


def _inproj(x2, mod3, mod_row_fn, g, w_t, b_t, cw_t, wg, bg, n_conv, period, tm):
    t, d = x2.shape
    nj = w_t.shape[0]
    const1 = lambda shape: pl.BlockSpec(shape, lambda i: (0,) * len(shape), pipeline_mode=pl.Buffered(1))
    return pl.pallas_call(
        functools.partial(_inproj_kernel, n_conv, period),
        grid=(t // tm,),
        in_specs=[pl.BlockSpec((tm, d), lambda i: (i, 0)),
                  pl.BlockSpec((None, 1, mod3.shape[2]), lambda i: (mod_row_fn(i), 0, 0)),
                  const1((1, d)), const1(w_t.shape), const1(b_t.shape), const1(cw_t.shape),
                  const1((d, LANES)), const1((1, LANES))],
        out_specs=[pl.BlockSpec((nj, tm, d), lambda i: (0, i, 0)),
                   pl.BlockSpec((M_HEADS, SUBLANES, tm), lambda i: (0, 0, i))],
        out_shape=[jax.ShapeDtypeStruct((nj, t, d), BF16),
                   jax.ShapeDtypeStruct((M_HEADS, SUBLANES, t), F32)],
        scratch_shapes=[pltpu.VMEM((tm, d), BF16)],
        compiler_params=_cparams(("parallel",)),
        name="inproj",
    )(x2, mod3, g, w_t, b_t, cw_t, wg, bg)


def _scan_step(q, k, v, gates, ctn_ref, m_ref, forward, want_out):
    t = k.shape[0]
    row0 = 0 if forward else 2
    i_row = gates[row0:row0 + 1, :]
    b_row = gates[row0 + 1:row0 + 2, :]
    b_last = b_row[:, t - 1:t] if forward else b_row[:, 0:1]
    m = m_ref[...]
    r = i_row - b_row
    m_new = b_last + jnp.maximum(m, jnp.max(r, axis=1, keepdims=True))
    k_t = k.T
    v1 = jnp.concatenate([v, jnp.ones((t, LANES), v.dtype)], axis=1)
    hout = None
    if want_out:
        tt = lax.broadcasted_iota(jnp.int32, (t, t), 0)
        ss = lax.broadcasted_iota(jnp.int32, (t, t), 1)
        allowed = (ss <= tt) if forward else (ss >= tt)
        m1 = jnp.where(allowed, r, NEG)
        cm = jnp.broadcast_to(jnp.maximum(m, jnp.max(m1, axis=1, keepdims=True)), (t, LANES))
        b_col = jnp.broadcast_to(jnp.sum(jnp.where(ss == tt, b_row, 0.0), axis=1, keepdims=True), (t, LANES))
        wide = lambda u, n: jnp.concatenate([u] * (n // LANES), axis=1)
        s = _dot(q, k_t) * jnp.exp(m1 - wide(cm, t))
        a = jnp.exp(m - cm)
        tot = _dot(s.astype(BF16), v1) + wide(a, HEAD_DIM + LANES) * _dot(q, ctn_ref[...].astype(BF16))
        den = tot[:, HEAD_DIM:]
        rden = 1.0 / jnp.maximum(jnp.abs(den), jnp.exp(-(b_col + cm)) * (HEAD_DIM ** 0.5))
        hout = tot[:, :HEAD_DIM] * wide(rden, HEAD_DIM)
    w_row = jnp.exp(b_last - m_new + r)
    decay = jnp.exp(b_last + m - m_new)
    kw = (k_t.astype(F32) * w_row).astype(BF16)
    ctn_ref[...] = decay * ctn_ref[...] + _dot(kw, v1)
    m_ref[...] = m_new
    return hout


def _mlstm_kernel(q_ref, k_ref, v_ref, o_ref, kc_ref, vc_ref, gr_ref, grc_ref, g_ref,
                  out_ref, hf_ref, hb_ref, ct_ref, m_ref):
    n_chunks = q_ref.shape[0] // SCAN_CHUNK
    t = SCAN_CHUNK
    hd = HEAD_DIM
    ct_ref[...] = jnp.zeros_like(ct_ref)
    m_ref[...] = jnp.zeros_like(m_ref)
    head_cols = [slice(h * hd, (h + 1) * hd) for h in range(MLSTM_HEADS)]

    for h, cols in enumerate(head_cols):
        for d, forward in enumerate((True, False)):
            _scan_step(None, kc_ref[:, cols], vc_ref[:, cols], grc_ref[h], ct_ref.at[h, d], m_ref.at[h, d],
                       forward, False)

    def body(c, carry):
        rf = pl.ds(pl.multiple_of(c * t, t), t)
        rb = pl.ds(pl.multiple_of((n_chunks - 1 - c) * t, t), t)
        for h, cols in enumerate(head_cols):
            hf_ref[rf, cols] = _scan_step(q_ref[rf, cols], k_ref[rf, cols], v_ref[rf, cols], gr_ref[h, :, rf],
                                          ct_ref.at[h, 0], m_ref.at[h, 0], True, True)
            hb_ref[rb, cols] = _scan_step(q_ref[rb, cols], k_ref[rb, cols], v_ref[rb, cols], gr_ref[h, :, rb],
                                          ct_ref.at[h, 1], m_ref.at[h, 1], False, True)
        return carry

    lax.fori_loop(0, n_chunks, body, 0)

    ones = jnp.ones((hd, LANES), BF16)
    wide = lambda u: jnp.concatenate([u] * (hd // LANES), axis=1)
    for cols in head_cols:
        hs = hf_ref[:, cols] + hb_ref[:, cols]
        mu = _dot(hs.astype(BF16), ones) * (1.0 / hd)
        dv = hs - wide(mu)
        var = _dot((dv * dv).astype(BF16), ones) * (1.0 / hd)
        y = dv * wide(lax.rsqrt(var + EPS)) * g_ref[:, cols] * _sigmoid(o_ref[:, cols].astype(F32))
        out_ref[:, cols] = y.astype(out_ref.dtype)


def _mlstm(streams, ctx_streams, gr, grc, m_norm_g, bsz, seq, ctx_len):
    hd = HEAD_DIM
    wd = MLSTM_HEADS * hd

    def stream(jidx):
        return pl.BlockSpec((None, seq, wd), lambda b, h: (jidx, b, h))

    def cstream(jidx):
        return pl.BlockSpec((None, ctx_len, wd), lambda b, h: (jidx, b, h))

    return pl.pallas_call(
        _mlstm_kernel,
        grid=(bsz, M_HEADS // MLSTM_HEADS),
        in_specs=[stream(J_Q), stream(J_K), stream(J_V), stream(J_O), cstream(0), cstream(1),
                  pl.BlockSpec((MLSTM_HEADS, SUBLANES, seq), lambda b, h: (h, 0, b)),
                  pl.BlockSpec((MLSTM_HEADS, SUBLANES, ctx_len), lambda b, h: (h, 0, b)),
                  pl.BlockSpec((1, wd), lambda b, h: (0, h))],
        out_specs=pl.BlockSpec((seq, wd), lambda b, h: (b, h)),
        out_shape=jax.ShapeDtypeStruct((bsz * seq, D_MODEL), BF16),
        scratch_shapes=[pltpu.VMEM((seq, wd), F32), pltpu.VMEM((seq, wd), F32),
                        pltpu.VMEM((MLSTM_HEADS, 2, hd, hd + LANES), F32),
                        pltpu.VMEM((MLSTM_HEADS, 2, 1, 1), F32)],
        compiler_params=_cparams(("parallel", "parallel")),
        name="mlstm",
    )(streams, streams, streams, streams, ctx_streams, ctx_streams, gr, grc, m_norm_g)


def _filt_kernel(z_ref, w1_ref, b1_ref, w2_ref, b2_ref, w3_ref, b3_ref, fr_ref, w4_ref, dec_ref, e_ref, d_ref):
    hp = lax.Precision.HIGHEST
    n = z_ref.shape[0]
    c = e_ref.shape[1]
    fr = fr_ref[...]
    a = jnp.sin(fr * (jnp.dot(z_ref[...], w1_ref[...], preferred_element_type=F32, precision=hp) + b1_ref[...]))
    a = jnp.sin(fr * (jnp.dot(a, w2_ref[...], preferred_element_type=F32, precision=hp) + b2_ref[...]))
    a = jnp.sin(fr * (jnp.dot(a, w3_ref[...], preferred_element_type=F32, precision=hp) + b3_ref[...]))
    taps = _dot(a.astype(BF16), w4_ref[...].astype(BF16))
    tcol = z_ref[:, 0:1]
    kf = taps[:, :c] * jnp.exp(-tcol * jnp.abs(dec_ref[0:1, :]))
    kb = taps[:, c:] * jnp.exp(-tcol * jnp.abs(dec_ref[1:2, :]))
    row0 = lax.broadcasted_iota(jnp.int32, (n, 1), 0) == 0
    hplus = jnp.where(row0, kf + kb, kf)
    hminus = jnp.where(row0, 0.0, kb)
    e_ref[...] = hplus + hminus
    d_ref[...] = hplus - hminus


def _filters(z, w1, b1, w2, b2, w3, b3, fr, w4, dec):
    n = z.shape[0]
    c = D_MODEL
    fw = w1.shape[1]
    full = lambda shape: pl.BlockSpec(shape, lambda o: (0,) * len(shape))
    return pl.pallas_call(
        _filt_kernel,
        grid=(HY_ORDER,),
        in_specs=[full(z.shape), full(w1.shape), full(b1.shape), full(w2.shape), full(b2.shape),
                  full(w3.shape), full(b3.shape), full(fr.shape),
                  pl.BlockSpec((fw, 2 * c), lambda o: (0, o)),
                  pl.BlockSpec((None, 2, c), lambda o: (o, 0, 0))],
        out_specs=[pl.BlockSpec((None, n, c), lambda o: (o, 0, 0)),
                   pl.BlockSpec((None, n, c), lambda o: (o, 0, 0))],
        out_shape=[jax.ShapeDtypeStruct((HY_ORDER, n, c), F32)] * 2,
        compiler_params=_cparams(("arbitrary",)),
        name="filt",
    )(z, w1, b1, w2, b2, w3, b3, fr, w4, dec)


FFT_R = 16
FFT_HALF = FFT_R // 2


def _axpy(acc, coef, x, memo):
    coef = round(coef, 15)
    if abs(coef) < 1e-12:
        return acc
    mag = abs(coef)
    if mag == 1.0:
        prod = x
    else:
        key = (id(x), mag)
        if key not in memo:
            memo[key] = (x, mag * x)
        prod = memo[key][1]
    if acc is None:
        return prod if coef > 0 else -prod
    return acc + prod if coef > 0 else acc - prod


def _stage_a_fwd(k1, zs, ssum, sdif, memo):
    even = k1 % 2 == 0
    re = _axpy(zs[0], math.cos(math.pi * k1 / 2), zs[FFT_HALF // 2], memo)
    im = _axpy(None, -math.sin(math.pi * k1 / 2), zs[FFT_HALF // 2], memo)
    for n in range(1, FFT_HALF // 2):
        th = 2 * math.pi * n * k1 / FFT_R
        re = _axpy(re, math.cos(th), ssum[n] if even else sdif[n], memo)
        im = _axpy(im, -math.sin(th), sdif[n] if even else ssum[n], memo)
    return re, im


class _StageAInverse:
    def __init__(self):
        self.y0 = None
        self.y4 = None
        self.t1 = [None] * (FFT_HALF // 2)
        self.t2 = [None] * (FFT_HALF // 2)

    def add(self, k1, vre, vim):
        even = k1 % 2 == 0
        memo = {}
        self.y0 = _axpy(self.y0, 1.0, vre, memo)
        self.y4 = _axpy(self.y4, math.cos(math.pi * k1 / 2), vre, memo)
        if vim is not None:
            self.y4 = _axpy(self.y4, -math.sin(math.pi * k1 / 2), vim, memo)
        for n in range(1, FFT_HALF // 2):
            th = 2 * math.pi * n * k1 / FFT_R
            if even:
                self.t1[n] = _axpy(self.t1[n], math.cos(th), vre, memo)
                if vim is not None:
                    self.t2[n] = _axpy(self.t2[n], -math.sin(th), vim, memo)
            else:
                self.t2[n] = _axpy(self.t2[n], math.cos(th), vre, memo)
                if vim is not None:
                    self.t1[n] = _axpy(self.t1[n], -math.sin(th), vim, memo)

    def slabs(self):
        ys = [None] * FFT_HALF
        ys[0], ys[FFT_HALF // 2] = self.y0, self.y4
        for n in range(1, FFT_HALF // 2):
            ys[n] = self.t1[n] + self.t2[n]
            ys[FFT_HALF - n] = self.t1[n] - self.t2[n]
        return ys


def _slabs(ref, m):
    zs = [ref[i * m:(i + 1) * m, :] for i in range(FFT_HALF)]
    ssum = [None] + [zs[n] + zs[FFT_HALF - n] for n in range(1, FFT_HALF // 2)]
    sdif = [None] + [zs[n] - zs[FFT_HALF - n] for n in range(1, FFT_HALF // 2)]
    return zs, ssum, sdif


def _stage_b_fwd(gb_ref, k1, re, im):
    m = re.shape[0]
    if im is None:
        return _dot(gb_ref[k1, :, 0:m], re.astype(BF16))
    return _dot(gb_ref[k1], jnp.concatenate([re, im], axis=0).astype(BF16))


def _spec_kernel(gb_ref, e_ref, d_ref, kre_ref, kim_ref):
    m = e_ref.shape[0] // FFT_HALF
    es = _slabs(e_ref, m)
    ds = _slabs(d_ref, m)
    memo_e, memo_d = {}, {}
    for k1 in range(FFT_HALF + 1):
        w = (1.0 if k1 in (0, FFT_HALF) else 2.0) / (FFT_R * m)
        kre_ref[k1] = _stage_b_fwd(gb_ref, k1, *_stage_a_fwd(k1, *es, memo_e))[0:m] * w
        kim_ref[k1] = _stage_b_fwd(gb_ref, k1, *_stage_a_fwd(k1, *ds, memo_d))[m:2 * m] * w


def _spectra(gb, e, d, ct):
    n = e.shape[1]
    c = e.shape[2]
    m = n // FFT_HALF
    tile = pl.BlockSpec((None, n, ct), lambda o, j: (o, 0, j))
    ktile = pl.BlockSpec((None, FFT_HALF + 1, m, ct), lambda o, j: (o, 0, 0, j))
    kshape = jax.ShapeDtypeStruct((HY_ORDER, FFT_HALF + 1, m, c), F32)
    return pl.pallas_call(
        _spec_kernel,
        grid=(HY_ORDER, c // ct),
        in_specs=[pl.BlockSpec(gb.shape, lambda o, j: (0, 0, 0)), tile, tile],
        out_specs=[ktile, ktile],
        out_shape=[kshape, kshape],
        compiler_params=_cparams(("arbitrary", "arbitrary")),
        name="spec",
    )(gb, e, d)


def _hyena_kernel(gb_ref, hb_ref, kre_ref, kim_ref, skip_ref, cw_ref, v_ref, x1_ref, x2_ref, o_ref, z_ref):
    n = z_ref.shape[0]
    m = n // FFT_HALF
    z_ref[...] = _token_conv(v_ref[...].astype(F32), cw_ref[0], GRID_W)
    gates = (x1_ref, x2_ref)
    for o in range(HY_ORDER):
        zs, ssum, sdif = _slabs(z_ref, m)
        inv = _StageAInverse()
        memo = {}
        for k1 in range(FFT_HALF + 1):
            re, im = _stage_a_fwd(k1, zs, ssum, sdif, memo)
            x = _stage_b_fwd(gb_ref, k1, re, im)
            xre, xim = x[0:m], x[m:2 * m]
            kre, kim = kre_ref[o, k1], kim_ref[o, k1]
            p = jnp.concatenate([xre * kre - xim * kim, xre * kim + xim * kre], axis=0).astype(BF16)
            if im is None:
                inv.add(k1, _dot(hb_ref[k1, 0:m, :], p), None)
            else:
                v = _dot(hb_ref[k1], p)
                inv.add(k1, v[0:m], v[m:2 * m])
        ys = inv.slabs()
        gate = _token_conv(gates[o][...].astype(F32), cw_ref[o + 1], GRID_W)
        for i in range(FFT_HALF):
            rows = slice(i * m, (i + 1) * m)
            z_ref[rows, :] = gate[rows] * (ys[i] + skip_ref[o:o + 1, :] * zs[i])
    o_ref[...] = z_ref[...].astype(o_ref.dtype)


def _hyena(gb, hb, kre, kim, skip, cw_h, streams, bsz, seq, ct):
    c = D_MODEL
    m = seq // FFT_HALF
    const1 = lambda shape: pl.BlockSpec(shape, lambda j, b: (0,) * len(shape), pipeline_mode=pl.Buffered(1))
    ktile = pl.BlockSpec((HY_ORDER, FFT_HALF + 1, m, ct), lambda j, b: (0, 0, 0, j), pipeline_mode=pl.Buffered(1))

    rows = seq

    def stream(jidx):
        return pl.BlockSpec((None, rows, ct), lambda j, b: (jidx, b, j))

    return pl.pallas_call(
        _hyena_kernel,
        grid=(c // ct, bsz),
        in_specs=[const1(gb.shape), const1(hb.shape), ktile, ktile,
                  pl.BlockSpec((HY_ORDER, ct), lambda j, b: (0, j)),
                  pl.BlockSpec((HY_ORDER + 1, SUBLANES, ct), lambda j, b: (0, 0, j)),
                  stream(J_HV), stream(J_HX1), stream(J_HX2)],
        out_specs=pl.BlockSpec((rows, ct), lambda j, b: (b, j)),
        out_shape=jax.ShapeDtypeStruct((bsz * seq, c), BF16),
        scratch_shapes=[pltpu.VMEM((rows, ct), F32)],
        compiler_params=_cparams(("parallel", "parallel")),
        name="hyena",
    )(gb, hb, kre, kim, skip, cw_h, streams, streams, streams)


def _tail_kernel(x_ref, mod_ref, hm_ref, hh_ref, gm_ref, gh_ref, wpm_ref, wph_ref, wout_ref,
                 n2_ref, w1_ref, w3_ref, w2_ref, fg_ref, o_ref):
    d = D_MODEL
    g1 = mod_ref[:, 2 * d:3 * d]
    sh2 = mod_ref[:, 3 * d:4 * d]
    sc2 = mod_ref[:, 4 * d:5 * d]
    g2 = mod_ref[:, 5 * d:6 * d]
    y = (_sigmoid(gm_ref[...].astype(F32)) * _dot(hm_ref[...], wpm_ref[...])
         + _sigmoid(gh_ref[...].astype(F32)) * _dot(hh_ref[...], wph_ref[...]))
    x1 = x_ref[...] + g1 * _dot(y.astype(BF16), wout_ref[...])
    h2 = _norm_mod(x1, n2_ref[...], sc2, sh2).astype(BF16)
    u = _dot(h2, w1_ref[...])
    a = (u * _sigmoid(u) * _dot(h2, w3_ref[...])).astype(BF16)
    x2 = x1 + g2 * _dot(a, w2_ref[...])
    o_ref[...] = x2 * lax.rsqrt(jnp.mean(x2 * x2, axis=-1, keepdims=True) + EPS) * fg_ref[...]


def _tail(x2d, mod3, hm, hh, streams, wpm, wph, wout, n2g, w1, w3, w2, fg, seq, tm):
    t, d = x2d.shape
    tiles_per_b = seq // tm
    const1 = lambda shape: pl.BlockSpec(shape, lambda i: (0,) * len(shape), pipeline_mode=pl.Buffered(1))
    rows = pl.BlockSpec((tm, d), lambda i: (i, 0))

    def stream(jidx):
        return pl.BlockSpec((None, tm, d), lambda i: (jidx, i, 0))

    return pl.pallas_call(
        _tail_kernel,
        grid=(t // tm,),
        in_specs=[rows, pl.BlockSpec((None, 1, mod3.shape[2]), lambda i: (i // tiles_per_b, 0, 0)),
                  rows, rows, stream(J_GM), stream(J_GH),
                  const1(wpm.shape), const1(wph.shape), const1(wout.shape), const1(n2g.shape),
                  const1(w1.shape), const1(w3.shape), const1(w2.shape), const1(fg.shape)],
        out_specs=rows,
        out_shape=jax.ShapeDtypeStruct((t, d), F32),
        compiler_params=_cparams(("parallel",)),
        name="tail",
    )(x2d, mod3, hm, hh, streams, streams, wpm, wph, wout, n2g, w1, w3, w2, fg)


def _stage_b_tables(n):
    m = n // FFT_HALF
    k1 = lax.broadcasted_iota(jnp.int32, (FFT_HALF + 1, m, m), 0)
    k2 = lax.broadcasted_iota(jnp.int32, (FFT_HALF + 1, m, m), 1)
    n2 = lax.broadcasted_iota(jnp.int32, (FFT_HALF + 1, m, m), 2)
    ang = ((n2 * (k1 + FFT_R * k2)) % (2 * n)).astype(F32) * (math.pi / n)
    cs, sn = jnp.cos(ang), jnp.sin(ang)
    gb = jnp.concatenate([jnp.concatenate([cs, sn], axis=2), jnp.concatenate([-sn, cs], axis=2)], axis=1)
    return gb.astype(BF16), jnp.swapaxes(gb, 1, 2).astype(BF16)


def _pos_embedding(n):
    t = jnp.linspace(0.0, 1.0, n, dtype=F32)[:, None]
    bands = jnp.linspace(1e-4, HY_BANDS - 1, HY_BANDS, dtype=F32)
    ang = (2.0 * math.pi / n) * jnp.arange(n, dtype=F32)[:, None] * bands[None, :]
    return jnp.concatenate([t, jnp.cos(ang), -jnp.sin(ang)], axis=-1)


def kernel(x, c, ctx, c_ctx, w_ada, b_ada, norm1_g, w_in, b_in, kq_conv_w, kq_conv_b, m_norm_g, hy_conv_w, hy_conv_b, hy_w1, hy_b1, hy_w2, hy_b2, hy_w3, hy_b3, hy_w4, hy_freq, hy_decay, hy_skip, w_pm, w_ph, w_out, norm2_g, ffn_w1, ffn_w3, ffn_w2, final_g):
    assert w_ada.shape[0] == 1, "single trunk layer"
    bsz, seq, d = x.shape
    ctx_len = ctx.shape[1]
    assert d == D_MODEL and seq % SCAN_CHUNK == 0 and ctx_len == SCAN_CHUNK
    row = lambda v: v.reshape(1, -1)

    wi, bi = w_in[0], b_in[0]
    col = lambda off: slice(off, off + d)
    order = [col(0), col(Q_OFF), col(HY_OFF), col(HY_OFF + d), col(HY_OFF + 2 * d),
             col(d), col(O_OFF), col(MG_OFF), col(MG_OFF + d)]
    w_t = jnp.stack([wi[:, s] for s in order]).astype(BF16)
    b_t = jnp.stack([bi[s] for s in order])[:, None, :]
    kqw, kqb, hyw, hyb = kq_conv_w[0], kq_conv_b[0], hy_conv_w[0], hy_conv_b[0]
    ones = jnp.ones((d,), F32)
    zeros = jnp.zeros((SUBLANES - 5, d), F32)

    def conv_rows(w3, cb, scale):
        return jnp.concatenate([w3, cb[None], (scale * ones)[None], zeros], axis=0)

    cw_kq = jnp.stack([conv_rows(kqw[:, :d], kqb[:d], 1.0), conv_rows(kqw[:, d:], kqb[d:], 1.0)])
    cw_h = jnp.stack([conv_rows(hyw[:, col(i * d)], hyb[col(i * d)], 1.0) for i in range(3)])
    ctx_sel = jnp.array([J_K, J_V])
    gsrc = jnp.array([G_OFF + g * M_HEADS + h for h in range(M_HEADS) for g in range(4)])
    gdst = jnp.array([SUBLANES * h + g for h in range(M_HEADS) for g in range(4)])
    wg = jnp.zeros((d, LANES), F32).at[:, gdst].set(wi[:, gsrc]).astype(BF16)
    bg = jnp.zeros((1, LANES), F32).at[0, gdst].set(bi[gsrc])

    n_mod = ((bsz + 1 + 7) // 8) * 8
    cs = jnp.zeros((n_mod, d), F32).at[:bsz].set(c).at[bsz].set(c_ctx)
    mod3 = _mod(cs, w_ada[0], row(b_ada[0]))[:, None, :]

    x2 = x.reshape(bsz * seq, d)
    ctx2 = ctx.reshape(bsz * ctx_len, d)
    g1n = row(norm1_g[0])
    tm = min(INPROJ_ROWS, seq)
    tmc = min(INPROJ_ROWS, bsz * ctx_len)
    lat_row = lambda i: i // (seq // tm)
    ctx_row = lambda i: bsz

    streams, gr = _inproj(x2, mod3, lat_row, g1n, w_t, b_t, cw_kq, wg, bg, N_CONV_TILES, GRID_W, tm)
    ctx_streams, grc = _inproj(ctx2, mod3, ctx_row, g1n, w_t[ctx_sel], b_t[ctx_sel], cw_kq[:1], wg, bg,
                               1, ctx_len, tmc)
    hm = _mlstm(streams, ctx_streams, gr, grc, row(m_norm_g[0]), bsz, seq, ctx_len)

    zemb = jnp.pad(_pos_embedding(seq), ((0, 0), (0, 7)))
    w1p = jnp.pad(hy_w1[0], ((0, 7), (0, 0)))
    e, dd = _filters(zemb, w1p, row(hy_b1[0]), hy_w2[0], row(hy_b2[0]), hy_w3[0], row(hy_b3[0]),
                     row(hy_freq[0]), hy_w4[0], hy_decay[0])
    gb, hb = _stage_b_tables(seq)
    kre, kim = _spectra(gb, e, dd, HY_CHANNELS)
    hh = _hyena(gb, hb, kre, kim, hy_skip[0], cw_h, streams, bsz, seq, HY_CHANNELS)

    out = _tail(x2, mod3, hm, hh, streams, w_pm[0].astype(BF16), w_ph[0].astype(BF16), w_out[0].astype(BF16),
                row(norm2_g[0]), ffn_w1[0].astype(BF16), ffn_w3[0].astype(BF16), ffn_w2[0].astype(BF16),
                row(final_g), seq, min(TAIL_ROWS, seq))
    return out.reshape(bsz, seq, d)
```

```python
import functools
import math

import jax
import jax.numpy as jnp
from jax import lax
from jax.experimental import pallas as pl
from jax.experimental.pallas import tpu as pltpu

F32 = jnp.float32
BF16 = jnp.bfloat16

D_MODEL = 1024
M_HEADS = 4
HEAD_DIM = D_MODEL // M_HEADS
GRID_W = 64
SCAN_CHUNK = 256
MLSTM_HEADS = 2
HY_ORDER = 2
HY_EMB = 33
HY_BANDS = (HY_EMB - 1) // 2
INPROJ_ROWS = 512
TAIL_ROWS = 512
HY_CHANNELS = 256
D_FF = ((8 * D_MODEL + 3 * 256 - 1) // (3 * 256)) * 256
EPS = 1e-6
NEG = -1e30
LANES = 128
SUBLANES = 8
VMEM_LIMIT = 56 * 1024 * 1024

G_OFF = 2 * D_MODEL
Q_OFF = G_OFF + 4 * M_HEADS
O_OFF = Q_OFF + D_MODEL
HY_OFF = O_OFF + D_MODEL
MG_OFF = HY_OFF + 3 * D_MODEL

J_K, J_Q, J_HV, J_HX1, J_HX2, J_V, J_O, J_GM, J_GH = range(9)
N_STREAMS = 9
N_CONV_TILES = 2
CONV_COL_BLOCKS = 4
GATE_ROWS = ("i_fwd", "cum_f_fwd", "i_bwd", "revcum_f_bwd")


def _cparams(sem):
    return pltpu.CompilerParams(dimension_semantics=sem, vmem_limit_bytes=VMEM_LIMIT)


def _dot(a, b):
    return jnp.dot(a, b, preferred_element_type=F32)


def _sigmoid(x):
    return 1.0 / (1.0 + jnp.exp(-x))


def _norm_mod(x, g, sc, sh):
    y = x * lax.rsqrt(jnp.mean(x * x, axis=-1, keepdims=True) + EPS) * g
    return y * (1.0 + sc) + sh


def _token_conv(p, taps, period):
    n, c = p.shape
    g = p.reshape(n // period, period, c)
    zero = jnp.zeros((n // period, 1, c), p.dtype)
    prev = jnp.concatenate([zero, g[:, :-1, :]], axis=1)
    nxt = jnp.concatenate([g[:, 1:, :], zero], axis=1)
    y = taps[0:1] * prev + taps[1:2] * g + taps[2:3] * nxt + taps[3:4]
    return y.reshape(n, c)


def _mod_kernel(c_ref, w_ref, b_ref, o_ref):
    c = c_ref[...]
    s = (c * _sigmoid(c)).astype(BF16)
    o_ref[...] = _dot(s, w_ref[...].astype(BF16)) + b_ref[...]


def _mod(cs, w_ada, b_ada):
    rows, d = cs.shape
    n = w_ada.shape[1]
    tn = 1024
    return pl.pallas_call(
        _mod_kernel,
        grid=(n // tn,),
        in_specs=[pl.BlockSpec((rows, d), lambda j: (0, 0)),
                  pl.BlockSpec((d, tn), lambda j: (0, j)),
                  pl.BlockSpec((1, tn), lambda j: (0, j))],
        out_specs=pl.BlockSpec((rows, tn), lambda j: (0, j)),
        out_shape=jax.ShapeDtypeStruct((rows, n), F32),
        compiler_params=_cparams(("arbitrary",)),
        name="mod",
    )(cs, w_ada, b_ada)


def _chunk_gates(pg):
    t = pg.shape[0]
    lane = lax.broadcasted_iota(jnp.int32, (1, pg.shape[1]), 1) % SUBLANES
    tt = lax.broadcasted_iota(jnp.int32, (t, t), 0)
    ss = lax.broadcasted_iota(jnp.int32, (t, t), 1)
    ltri = jnp.where(ss <= tt, 1.0, 0.0).astype(BF16)
    hi = pg.astype(BF16)
    r1 = pg - hi.astype(F32)
    mid = r1.astype(BF16)
    lo = (r1 - mid.astype(F32)).astype(BF16)
    cf = _dot(ltri, hi) + _dot(ltri, mid) + _dot(ltri, lo)
    rev = cf[t - 1:t, :] - cf + pg
    return jnp.where(lane == 1, cf, jnp.where(lane == 3, rev, pg))


def _inproj_kernel(n_conv, period, x_ref, mod_ref, g_ref, w_ref, b_ref, cw_ref, wg_ref, bg_ref,
                   o_ref, gr_ref, h_ref):
    tm = x_ref.shape[0]
    nj = w_ref.shape[0]
    h = _norm_mod(x_ref[...], g_ref[...], mod_ref[:, D_MODEL:2 * D_MODEL], mod_ref[:, 0:D_MODEL]).astype(BF16)
    h_ref[...] = h
    pg = _dot(h, wg_ref[...]) + bg_ref[...]
    lane = lax.broadcasted_iota(jnp.int32, (1, pg.shape[1]), 1) % SUBLANES
    log_f = jnp.minimum(pg, 0.0) - jnp.log(1.0 + jnp.exp(-jnp.abs(pg)))
    pg = jnp.where((lane == 1) | (lane == 3), log_f, pg)
    for c in range(tm // SCAN_CHUNK):
        cols = slice(c * SCAN_CHUNK, (c + 1) * SCAN_CHUNK)
        res_t = _chunk_gates(pg[cols]).T
        for hd in range(M_HEADS):
            gr_ref[hd, :, cols] = res_t[hd * SUBLANES:(hd + 1) * SUBLANES, :]

    nb = D_MODEL // CONV_COL_BLOCKS
    pending = []
    for j in range(n_conv):
        for n in range(CONV_COL_BLOCKS):
            cols = slice(n * nb, (n + 1) * nb)
            pending.append((j, cols, _dot(h_ref[...], w_ref[j, :, cols]) + b_ref[j, :, cols]))

    def finish(count):
        for _ in range(min(count, len(pending))):
            j, cols, acc = pending.pop(0)
            y = _token_conv(acc, cw_ref[j, :, cols], period)
            o_ref[j, :, cols] = (y * _sigmoid(y)).astype(o_ref.dtype)

    per_tile = -(-len(pending) // max(nj - n_conv, 1))
    for j in range(n_conv, nj):
        o_ref[j] = (_dot(h_ref[...], w_ref[j]) + b_ref[j]).astype(o_ref.dtype)
        finish(per_tile)
    finish(len(pending))


def _inproj(x2, mod3, mod_row_fn, g, w_t, b_t, cw_t, wg, bg, n_conv, period, tm):
    t, d = x2.shape
    nj = w_t.shape[0]
    const1 = lambda shape: pl.BlockSpec(shape, lambda i: (0,) * len(shape), pipeline_mode=pl.Buffered(1))
    return pl.pallas_call(
        functools.partial(_inproj_kernel, n_conv, period),
        grid=(t // tm,),
        in_specs=[pl.BlockSpec((tm, d), lambda i: (i, 0)),
                  pl.BlockSpec((None, 1, mod3.shape[2]), lambda i: (mod_row_fn(i), 0, 0)),
                  const1((1, d)), const1(w_t.shape), const1(b_t.shape), const1(cw_t.shape),
                  const1((d, LANES)), const1((1, LANES))],
        out_specs=[pl.BlockSpec((nj, tm, d), lambda i: (0, i, 0)),
                   pl.BlockSpec((M_HEADS, SUBLANES, tm), lambda i: (0, 0, i))],
        out_shape=[jax.ShapeDtypeStruct((nj, t, d), BF16),
                   jax.ShapeDtypeStruct((M_HEADS, SUBLANES, t), F32)],
        scratch_shapes=[pltpu.VMEM((tm, d), BF16)],
        compiler_params=_cparams(("parallel",)),
        name="inproj",
    )(x2, mod3, g, w_t, b_t, cw_t, wg, bg)


def _scan_step(q, k, v, gates, ctn_ref, m_ref, forward, want_out):
    t = k.shape[0]
    row0 = 0 if forward else 2
    i_row = gates[row0:row0 + 1, :]
    b_row = gates[row0 + 1:row0 + 2, :]
    b_last = b_row[:, t - 1:t] if forward else b_row[:, 0:1]
    m = m_ref[...]
    r = i_row - b_row
    m_new = b_last + jnp.maximum(m, jnp.max(r, axis=1, keepdims=True))
    k_t = k.T
    v1 = jnp.concatenate([v, jnp.ones((t, LANES), v.dtype)], axis=1)
    hout = None
    if want_out:
        tt = lax.broadcasted_iota(jnp.int32, (t, t), 0)
        ss = lax.broadcasted_iota(jnp.int32, (t, t), 1)
        allowed = (ss <= tt) if forward else (ss >= tt)
        m1 = jnp.where(allowed, r, NEG)
        cm = jnp.broadcast_to(jnp.maximum(m, jnp.max(m1, axis=1, keepdims=True)), (t, LANES))
        b_col = jnp.broadcast_to(jnp.sum(jnp.where(ss == tt, b_row, 0.0), axis=1, keepdims=True), (t, LANES))
        wide = lambda u, n: jnp.concatenate([u] * (n // LANES), axis=1)
        s = _dot(q, k_t) * jnp.exp(m1 - wide(cm, t))
        a = jnp.exp(m - cm)
        tot = _dot(s.astype(BF16), v1) + wide(a, HEAD_DIM + LANES) * _dot(q, ctn_ref[...].astype(BF16))
        den = tot[:, HEAD_DIM:]
        rden = 1.0 / jnp.maximum(jnp.abs(den), jnp.exp(-(b_col + cm)) * (HEAD_DIM ** 0.5))
        hout = tot[:, :HEAD_DIM] * wide(rden, HEAD_DIM)
    w_row = jnp.exp(b_last - m_new + r)
    decay = jnp.exp(b_last + m - m_new)
    kw = k_t * w_row.astype(BF16)
    ctn_ref[...] = decay * ctn_ref[...] + _dot(kw, v1)
    m_ref[...] = m_new
    return hout


def _mlstm_kernel(q_ref, k_ref, v_ref, o_ref, kc_ref, vc_ref, gr_ref, grc_ref, g_ref,
                  out_ref, hf_ref, hb_ref, ct_ref, m_ref):
    n_chunks = q_ref.shape[0] // SCAN_CHUNK
    t = SCAN_CHUNK
    hd = HEAD_DIM
    ct_ref[...] = jnp.zeros_like(ct_ref)
    m_ref[...] = jnp.zeros_like(m_ref)
    head_cols = [slice(h * hd, (h + 1) * hd) for h in range(MLSTM_HEADS)]

    for h, cols in enumerate(head_cols):
        for d, forward in enumerate((True, False)):
            _scan_step(None, kc_ref[:, cols], vc_ref[:, cols], grc_ref[h], ct_ref.at[h, d], m_ref.at[h, d],
                       forward, False)

    def body(c, carry):
        rf = pl.ds(pl.multiple_of(c * t, t), t)
        rb = pl.ds(pl.multiple_of((n_chunks - 1 - c) * t, t), t)
        for h, cols in enumerate(head_cols):
            hf_ref[rf, cols] = _scan_step(q_ref[rf, cols], k_ref[rf, cols], v_ref[rf, cols], gr_ref[h, :, rf],
                                          ct_ref.at[h, 0], m_ref.at[h, 0], True, True)
            hb_ref[rb, cols] = _scan_step(q_ref[rb, cols], k_ref[rb, cols], v_ref[rb, cols], gr_ref[h, :, rb],
                                          ct_ref.at[h, 1], m_ref.at[h, 1], False, True)
        return carry

    lax.fori_loop(0, n_chunks, body, 0)

    ones = jnp.ones((hd, LANES), BF16)
    wide = lambda u: jnp.concatenate([u] * (hd // LANES), axis=1)
    for cols in head_cols:
        hs = hf_ref[:, cols] + hb_ref[:, cols]
        mu = _dot(hs.astype(BF16), ones) * (1.0 / hd)
        dv = hs - wide(mu)
        var = _dot((dv * dv).astype(BF16), ones) * (1.0 / hd)
        y = dv * wide(lax.rsqrt(var + EPS)) * g_ref[:, cols] * _sigmoid(o_ref[:, cols].astype(F32))
        out_ref[:, cols] = y.astype(out_ref.dtype)


def _mlstm(streams, ctx_streams, gr, grc, m_norm_g, bsz, seq, ctx_len):
    hd = HEAD_DIM
    wd = MLSTM_HEADS * hd

    def stream(jidx):
        return pl.BlockSpec((None, seq, wd), lambda b, h: (jidx, b, h))

    def cstream(jidx):
        return pl.BlockSpec((None, ctx_len, wd), lambda b, h: (jidx, b, h))

    return pl.pallas_call(
        _mlstm_kernel,
        grid=(bsz, M_HEADS // MLSTM_HEADS),
        in_specs=[stream(J_Q), stream(J_K), stream(J_V), stream(J_O), cstream(0), cstream(1),
                  pl.BlockSpec((MLSTM_HEADS, SUBLANES, seq), lambda b, h: (h, 0, b)),
                  pl.BlockSpec((MLSTM_HEADS, SUBLANES, ctx_len), lambda b, h: (h, 0, b)),
                  pl.BlockSpec((1, wd), lambda b, h: (0, h))],
        out_specs=pl.BlockSpec((seq, wd), lambda b, h: (b, h)),
        out_shape=jax.ShapeDtypeStruct((bsz * seq, D_MODEL), BF16),
        scratch_shapes=[pltpu.VMEM((seq, wd), F32), pltpu.VMEM((seq, wd), F32),
                        pltpu.VMEM((MLSTM_HEADS, 2, hd, hd + LANES), F32),
                        pltpu.VMEM((MLSTM_HEADS, 2, 1, 1), F32)],
        compiler_params=_cparams(("parallel", "parallel")),
        name="mlstm",
    )(streams, streams, streams, streams, ctx_streams, ctx_streams, gr, grc, m_norm_g)


def _filt_kernel(z_ref, w1_ref, b1_ref, w2_ref, b2_ref, w3_ref, b3_ref, fr_ref, w4_ref, dec_ref, e_ref, d_ref):
    hp = lax.Precision.HIGHEST
    n = z_ref.shape[0]
    c = e_ref.shape[1]
    fr = fr_ref[...]
    a = jnp.sin(fr * (jnp.dot(z_ref[...], w1_ref[...], preferred_element_type=F32, precision=hp) + b1_ref[...]))
    a = jnp.sin(fr * (jnp.dot(a, w2_ref[...], preferred_element_type=F32, precision=hp) + b2_ref[...]))
    a = jnp.sin(fr * (jnp.dot(a, w3_ref[...], preferred_element_type=F32, precision=hp) + b3_ref[...]))
    taps = _dot(a.astype(BF16), w4_ref[...].astype(BF16))
    tcol = z_ref[:, 0:1]
    kf = taps[:, :c] * jnp.exp(-tcol * jnp.abs(dec_ref[0:1, :]))
    kb = taps[:, c:] * jnp.exp(-tcol * jnp.abs(dec_ref[1:2, :]))
    row0 = lax.broadcasted_iota(jnp.int32, (n, 1), 0) == 0
    hplus = jnp.where(row0, kf + kb, kf)
    hminus = jnp.where(row0, 0.0, kb)
    e_ref[...] = hplus + hminus
    d_ref[...] = hplus - hminus


def _filters(z, w1, b1, w2, b2, w3, b3, fr, w4, dec):
    n = z.shape[0]
    c = D_MODEL
    fw = w1.shape[1]
    full = lambda shape: pl.BlockSpec(shape, lambda o: (0,) * len(shape))
    return pl.pallas_call(
        _filt_kernel,
        grid=(HY_ORDER,),
        in_specs=[full(z.shape), full(w1.shape), full(b1.shape), full(w2.shape), full(b2.shape),
                  full(w3.shape), full(b3.shape), full(fr.shape),
                  pl.BlockSpec((fw, 2 * c), lambda o: (0, o)),
                  pl.BlockSpec((None, 2, c), lambda o: (o, 0, 0))],
        out_specs=[pl.BlockSpec((None, n, c), lambda o: (o, 0, 0)),
                   pl.BlockSpec((None, n, c), lambda o: (o, 0, 0))],
        out_shape=[jax.ShapeDtypeStruct((HY_ORDER, n, c), F32)] * 2,
        compiler_params=_cparams(("arbitrary",)),
        name="filt",
    )(z, w1, b1, w2, b2, w3, b3, fr, w4, dec)


FFT_R = 16
FFT_HALF = FFT_R // 2


def _axpy(acc, coef, x, memo):
    coef = round(coef, 15)
    if abs(coef) < 1e-12:
        return acc
    mag = abs(coef)
    if mag == 1.0:
        prod = x
    else:
        key = (id(x), mag)
        if key not in memo:
            memo[key] = (x, mag * x)
        prod = memo[key][1]
    if acc is None:
        return prod if coef > 0 else -prod
    return acc + prod if coef > 0 else acc - prod


def _stage_a_fwd(k1, zs, ssum, sdif, memo):
    even = k1 % 2 == 0
    re = _axpy(zs[0], math.cos(math.pi * k1 / 2), zs[FFT_HALF // 2], memo)
    im = _axpy(None, -math.sin(math.pi * k1 / 2), zs[FFT_HALF // 2], memo)
    for n in range(1, FFT_HALF // 2):
        th = 2 * math.pi * n * k1 / FFT_R
        re = _axpy(re, math.cos(th), ssum[n] if even else sdif[n], memo)
        im = _axpy(im, -math.sin(th), sdif[n] if even else ssum[n], memo)
    return re, im


class _StageAInverse:
    def __init__(self):
        self.y0 = None
        self.y4 = None
        self.t1 = [None] * (FFT_HALF // 2)
        self.t2 = [None] * (FFT_HALF // 2)

    def add(self, k1, vre, vim):
        even = k1 % 2 == 0
        memo = {}
        self.y0 = _axpy(self.y0, 1.0, vre, memo)
        self.y4 = _axpy(self.y4, math.cos(math.pi * k1 / 2), vre, memo)
        if vim is not None:
            self.y4 = _axpy(self.y4, -math.sin(math.pi * k1 / 2), vim, memo)
        for n in range(1, FFT_HALF // 2):
            th = 2 * math.pi * n * k1 / FFT_R
            if even:
                self.t1[n] = _axpy(self.t1[n], math.cos(th), vre, memo)
                if vim is not None:
                    self.t2[n] = _axpy(self.t2[n], -math.sin(th), vim, memo)
            else:
                self.t2[n] = _axpy(self.t2[n], math.cos(th), vre, memo)
                if vim is not None:
                    self.t1[n] = _axpy(self.t1[n], -math.sin(th), vim, memo)

    def slabs(self):
        ys = [None] * FFT_HALF
        ys[0], ys[FFT_HALF // 2] = self.y0, self.y4
        for n in range(1, FFT_HALF // 2):
            ys[n] = self.t1[n] + self.t2[n]
            ys[FFT_HALF - n] = self.t1[n] - self.t2[n]
        return ys


def _slabs(ref, m):
    zs = [ref[i * m:(i + 1) * m, :] for i in range(FFT_HALF)]
    ssum = [None] + [zs[n] + zs[FFT_HALF - n] for n in range(1, FFT_HALF // 2)]
    sdif = [None] + [zs[n] - zs[FFT_HALF - n] for n in range(1, FFT_HALF // 2)]
    return zs, ssum, sdif


def _stage_b_fwd(gb_ref, k1, re, im):
    m = re.shape[0]
    if im is None:
        return _dot(gb_ref[k1, :, 0:m], re.astype(BF16))
    return _dot(gb_ref[k1], jnp.concatenate([re, im], axis=0).astype(BF16))


def _spec_kernel(gb_ref, e_ref, d_ref, kre_ref, kim_ref):
    m = e_ref.shape[0] // FFT_HALF
    es = _slabs(e_ref, m)
    ds = _slabs(d_ref, m)
    memo_e, memo_d = {}, {}
    for k1 in range(FFT_HALF + 1):
        w = (1.0 if k1 in (0, FFT_HALF) else 2.0) / (FFT_R * m)
        kre_ref[k1] = (_stage_b_fwd(gb_ref, k1, *_stage_a_fwd(k1, *es, memo_e))[0:m] * w).astype(kre_ref.dtype)
        kim_ref[k1] = (_stage_b_fwd(gb_ref, k1, *_stage_a_fwd(k1, *ds, memo_d))[m:2 * m] * w).astype(kim_ref.dtype)


def _spectra(gb, e, d, ct):
    n = e.shape[1]
    c = e.shape[2]
    m = n // FFT_HALF
    tile = pl.BlockSpec((None, n, ct), lambda o, j: (o, 0, j))
    ktile = pl.BlockSpec((None, FFT_HALF + 1, m, ct), lambda o, j: (o, 0, 0, j))
    kshape = jax.ShapeDtypeStruct((HY_ORDER, FFT_HALF + 1, m, c), BF16)
    return pl.pallas_call(
        _spec_kernel,
        grid=(HY_ORDER, c // ct),
        in_specs=[pl.BlockSpec(gb.shape, lambda o, j: (0, 0, 0)), tile, tile],
        out_specs=[ktile, ktile],
        out_shape=[kshape, kshape],
        compiler_params=_cparams(("arbitrary", "arbitrary")),
        name="spec",
    )(gb, e, d)


def _hyena_kernel(gb_ref, hb_ref, kre_ref, kim_ref, skip_ref, cw_ref, v_ref, x1_ref, x2_ref, o_ref, z_ref):
    n = z_ref.shape[0]
    m = n // FFT_HALF
    z_ref[...] = _token_conv(v_ref[...], cw_ref[0].astype(BF16), GRID_W).astype(F32)
    gates = (x1_ref, x2_ref)
    for o in range(HY_ORDER):
        zs, ssum, sdif = _slabs(z_ref, m)
        inv = _StageAInverse()
        memo = {}
        for k1 in range(FFT_HALF + 1):
            re, im = _stage_a_fwd(k1, zs, ssum, sdif, memo)
            x = _stage_b_fwd(gb_ref, k1, re, im)
            xb = x.astype(BF16)
            xre, xim = xb[0:m], xb[m:2 * m]
            kre, kim = kre_ref[o, k1], kim_ref[o, k1]
            p = jnp.concatenate([xre * kre - xim * kim, xre * kim + xim * kre], axis=0)
            if im is None:
                inv.add(k1, _dot(hb_ref[k1, 0:m, :], p), None)
            else:
                v = _dot(hb_ref[k1], p)
                inv.add(k1, v[0:m], v[m:2 * m])
        ys = inv.slabs()
        gate = _token_conv(gates[o][...], cw_ref[o + 1].astype(BF16), GRID_W).astype(F32)
        for i in range(FFT_HALF):
            rows = slice(i * m, (i + 1) * m)
            z_ref[rows, :] = gate[rows] * (ys[i] + skip_ref[o:o + 1, :] * zs[i])
    o_ref[...] = z_ref[...].astype(o_ref.dtype)


def _hyena(gb, hb, kre, kim, skip, cw_h, streams, bsz, seq, ct):
    c = D_MODEL
    m = seq // FFT_HALF
    const1 = lambda shape: pl.BlockSpec(shape, lambda j, b: (0,) * len(shape), pipeline_mode=pl.Buffered(1))
    ktile = pl.BlockSpec((HY_ORDER, FFT_HALF + 1, m, ct), lambda j, b: (0, 0, 0, j), pipeline_mode=pl.Buffered(1))

    rows = seq

    def stream(jidx):
        return pl.BlockSpec((None, rows, ct), lambda j, b: (jidx, b, j))

    return pl.pallas_call(
        _hyena_kernel,
        grid=(c // ct, bsz),
        in_specs=[const1(gb.shape), const1(hb.shape), ktile, ktile,
                  pl.BlockSpec((HY_ORDER, ct), lambda j, b: (0, j)),
                  pl.BlockSpec((HY_ORDER + 1, SUBLANES, ct), lambda j, b: (0, 0, j)),
                  stream(J_HV), stream(J_HX1), stream(J_HX2)],
        out_specs=pl.BlockSpec((rows, ct), lambda j, b: (b, j)),
        out_shape=jax.ShapeDtypeStruct((bsz * seq, c), BF16),
        scratch_shapes=[pltpu.VMEM((rows, ct), F32)],
        compiler_params=_cparams(("parallel", "parallel")),
        name="hyena",
    )(gb, hb, kre, kim, skip, cw_h, streams, streams, streams)


def _tail_kernel(x_ref, mod_ref, hm_ref, hh_ref, gm_ref, gh_ref, wpm_ref, wph_ref, wout_ref,
                 n2_ref, w1_ref, w3_ref, w2_ref, fg_ref, o_ref):
    d = D_MODEL
    g1 = mod_ref[:, 2 * d:3 * d]
    sh2 = mod_ref[:, 3 * d:4 * d]
    sc2 = mod_ref[:, 4 * d:5 * d]
    g2 = mod_ref[:, 5 * d:6 * d]
    y = (_sigmoid(gm_ref[...].astype(F32)) * _dot(hm_ref[...], wpm_ref[...])
         + _sigmoid(gh_ref[...].astype(F32)) * _dot(hh_ref[...], wph_ref[...]))
    x1 = x_ref[...] + g1 * _dot(y.astype(BF16), wout_ref[...])
    h2 = _norm_mod(x1, n2_ref[...], sc2, sh2).astype(BF16)
    u = _dot(h2, w1_ref[...])
    a = (u * _sigmoid(u) * _dot(h2, w3_ref[...])).astype(BF16)
    x2 = x1 + g2 * _dot(a, w2_ref[...])
    o_ref[...] = x2 * lax.rsqrt(jnp.mean(x2 * x2, axis=-1, keepdims=True) + EPS) * fg_ref[...]


def _tail(x2d, mod3, hm, hh, streams, wpm, wph, wout, n2g, w1, w3, w2, fg, seq, tm):
    t, d = x2d.shape
    tiles_per_b = seq // tm
    const1 = lambda shape: pl.BlockSpec(shape, lambda i: (0,) * len(shape), pipeline_mode=pl.Buffered(1))
    rows = pl.BlockSpec((tm, d), lambda i: (i, 0))

    def stream(jidx):
        return pl.BlockSpec((None, tm, d), lambda i: (jidx, i, 0))

    return pl.pallas_call(
        _tail_kernel,
        grid=(t // tm,),
        in_specs=[rows, pl.BlockSpec((None, 1, mod3.shape[2]), lambda i: (i // tiles_per_b, 0, 0)),
                  rows, rows, stream(J_GM), stream(J_GH),
                  const1(wpm.shape), const1(wph.shape), const1(wout.shape), const1(n2g.shape),
                  const1(w1.shape), const1(w3.shape), const1(w2.shape), const1(fg.shape)],
        out_specs=rows,
        out_shape=jax.ShapeDtypeStruct((t, d), F32),
        compiler_params=_cparams(("parallel",)),
        name="tail",
    )(x2d, mod3, hm, hh, streams, streams, wpm, wph, wout, n2g, w1, w3, w2, fg)


def _stage_b_tables(n):
    m = n // FFT_HALF
    k1 = lax.broadcasted_iota(jnp.int32, (FFT_HALF + 1, m, m), 0)
    k2 = lax.broadcasted_iota(jnp.int32, (FFT_HALF + 1, m, m), 1)
    n2 = lax.broadcasted_iota(jnp.int32, (FFT_HALF + 1, m, m), 2)
    ang = ((n2 * (k1 + FFT_R * k2)) % (2 * n)).astype(F32) * (math.pi / n)
    cs, sn = jnp.cos(ang), jnp.sin(ang)
    gb = jnp.concatenate([jnp.concatenate([cs, sn], axis=2), jnp.concatenate([-sn, cs], axis=2)], axis=1)
    return gb.astype(BF16), jnp.swapaxes(gb, 1, 2).astype(BF16)


def _pos_embedding(n):
    t = jnp.linspace(0.0, 1.0, n, dtype=F32)[:, None]
    bands = jnp.linspace(1e-4, HY_BANDS - 1, HY_BANDS, dtype=F32)
    ang = (2.0 * math.pi / n) * jnp.arange(n, dtype=F32)[:, None] * bands[None, :]
    return jnp.concatenate([t, jnp.cos(ang), -jnp.sin(ang)], axis=-1)


def kernel(x, c, ctx, c_ctx, w_ada, b_ada, norm1_g, w_in, b_in, kq_conv_w, kq_conv_b, m_norm_g, hy_conv_w, hy_conv_b, hy_w1, hy_b1, hy_w2, hy_b2, hy_w3, hy_b3, hy_w4, hy_freq, hy_decay, hy_skip, w_pm, w_ph, w_out, norm2_g, ffn_w1, ffn_w3, ffn_w2, final_g):
    assert w_ada.shape[0] == 1, "single trunk layer"
    bsz, seq, d = x.shape
    ctx_len = ctx.shape[1]
    assert d == D_MODEL and seq % SCAN_CHUNK == 0 and ctx_len == SCAN_CHUNK
    row = lambda v: v.reshape(1, -1)

    wi, bi = w_in[0], b_in[0]
    col = lambda off: slice(off, off + d)
    order = [col(0), col(Q_OFF), col(HY_OFF), col(HY_OFF + d), col(HY_OFF + 2 * d),
             col(d), col(O_OFF), col(MG_OFF), col(MG_OFF + d)]
    w_t = jnp.stack([wi[:, s] for s in order]).astype(BF16)
    b_t = jnp.stack([bi[s] for s in order])[:, None, :]
    kqw, kqb, hyw, hyb = kq_conv_w[0], kq_conv_b[0], hy_conv_w[0], hy_conv_b[0]
    ones = jnp.ones((d,), F32)
    zeros = jnp.zeros((SUBLANES - 5, d), F32)

    def conv_rows(w3, cb, scale):
        return jnp.concatenate([w3, cb[None], (scale * ones)[None], zeros], axis=0)

    cw_kq = jnp.stack([conv_rows(kqw[:, :d], kqb[:d], 1.0), conv_rows(kqw[:, d:], kqb[d:], 1.0)])
    cw_h = jnp.stack([conv_rows(hyw[:, col(i * d)], hyb[col(i * d)], 1.0) for i in range(3)])
    ctx_sel = jnp.array([J_K, J_V])
    gsrc = jnp.array([G_OFF + g * M_HEADS + h for h in range(M_HEADS) for g in range(4)])
    gdst = jnp.array([SUBLANES * h + g for h in range(M_HEADS) for g in range(4)])
    wg = jnp.zeros((d, LANES), F32).at[:, gdst].set(wi[:, gsrc]).astype(BF16)
    bg = jnp.zeros((1, LANES), F32).at[0, gdst].set(bi[gsrc])

    n_mod = ((bsz + 1 + 7) // 8) * 8
    cs = jnp.zeros((n_mod, d), F32).at[:bsz].set(c).at[bsz].set(c_ctx)
    mod3 = _mod(cs, w_ada[0], row(b_ada[0]))[:, None, :]

    x2 = x.reshape(bsz * seq, d)
    ctx2 = ctx.reshape(bsz * ctx_len, d)
    g1n = row(norm1_g[0])
    tm = min(INPROJ_ROWS, seq)
    tmc = min(INPROJ_ROWS, bsz * ctx_len)
    lat_row = lambda i: i // (seq // tm)
    ctx_row = lambda i: bsz

    streams, gr = _inproj(x2, mod3, lat_row, g1n, w_t, b_t, cw_kq, wg, bg, N_CONV_TILES, GRID_W, tm)
    ctx_streams, grc = _inproj(ctx2, mod3, ctx_row, g1n, w_t[ctx_sel], b_t[ctx_sel], cw_kq[:1], wg, bg,
                               1, ctx_len, tmc)
    hm = _mlstm(streams, ctx_streams, gr, grc, row(m_norm_g[0]), bsz, seq, ctx_len)

    zemb = jnp.pad(_pos_embedding(seq), ((0, 0), (0, 7)))
    w1p = jnp.pad(hy_w1[0], ((0, 7), (0, 0)))
    e, dd = _filters(zemb, w1p, row(hy_b1[0]), hy_w2[0], row(hy_b2[0]), hy_w3[0], row(hy_b3[0]),
                     row(hy_freq[0]), hy_w4[0], hy_decay[0])
    gb, hb = _stage_b_tables(seq)
    kre, kim = _spectra(gb, e, dd, HY_CHANNELS)
    hh = _hyena(gb, hb, kre, kim, hy_skip[0], cw_h, streams, bsz, seq, HY_CHANNELS)

    out = _tail(x2, mod3, hm, hh, streams, w_pm[0].astype(BF16), w_ph[0].astype(BF16), w_out[0].astype(BF16),
                row(norm2_g[0]), ffn_w1[0].astype(BF16), ffn_w3[0].astype(BF16), ffn_w2[0].astype(BF16),
                row(final_g), seq, min(TAIL_ROWS, seq))
    return out.reshape(bsz, seq, d)
```

```python
import functools
import math

import jax
import jax.numpy as jnp
from jax import lax
from jax.experimental import pallas as pl
from jax.experimental.pallas import tpu as pltpu

F32 = jnp.float32
BF16 = jnp.bfloat16

D_MODEL = 1024
M_HEADS = 4
HEAD_DIM = D_MODEL // M_HEADS
GRID_W = 64
SCAN_CHUNK = 256
MLSTM_HEADS = 2
HY_ORDER = 2
HY_EMB = 33
HY_BANDS = (HY_EMB - 1) // 2
INPROJ_ROWS = 512
TAIL_ROWS = 512
HY_CHANNELS = 256
EPS = 1e-6
NEG = -1e30
LANES = 128
SUBLANES = 8
VMEM_LIMIT = 56 * 1024 * 1024

G_OFF = 2 * D_MODEL
Q_OFF = G_OFF + 4 * M_HEADS
O_OFF = Q_OFF + D_MODEL
HY_OFF = O_OFF + D_MODEL
MG_OFF = HY_OFF + 3 * D_MODEL

J_K, J_Q, J_HV, J_HX1, J_HX2, J_V, J_O, J_GM, J_GH = range(9)
N_CONV_TILES = 2
CONV_COL_BLOCKS = 4
GATE_ROWS = ("i_fwd", "cum_f_fwd", "i_bwd", "revcum_f_bwd")


def _cparams(sem):
    return pltpu.CompilerParams(dimension_semantics=sem, vmem_limit_bytes=VMEM_LIMIT)


def _dot(a, b):
    return jnp.dot(a, b, preferred_element_type=F32)


def _sigmoid(x):
    return 1.0 / (1.0 + jnp.exp(-x))


def _norm_mod(x, g, sc, sh):
    y = x * lax.rsqrt(jnp.mean(x * x, axis=-1, keepdims=True) + EPS) * g
    return y * (1.0 + sc) + sh


def _token_conv(p, taps, period):
    n, c = p.shape
    g = p.reshape(n // period, period, c)
    zero = jnp.zeros((n // period, 1, c), p.dtype)
    prev = jnp.concatenate([zero, g[:, :-1, :]], axis=1)
    nxt = jnp.concatenate([g[:, 1:, :], zero], axis=1)
    y = taps[0:1] * prev + taps[1:2] * g + taps[2:3] * nxt + taps[3:4]
    return y.reshape(n, c)


def _mod_kernel(c_ref, w_ref, b_ref, o_ref):
    c = c_ref[...]
    s = (c * _sigmoid(c)).astype(BF16)
    o_ref[...] = _dot(s, w_ref[...].astype(BF16)) + b_ref[...]


def _mod(cs, w_ada, b_ada):
    rows, d = cs.shape
    n = w_ada.shape[1]
    tn = 1024
    return pl.pallas_call(
        _mod_kernel,
        grid=(n // tn,),
        in_specs=[pl.BlockSpec((rows, d), lambda j: (0, 0)),
                  pl.BlockSpec((d, tn), lambda j: (0, j)),
                  pl.BlockSpec((1, tn), lambda j: (0, j))],
        out_specs=pl.BlockSpec((rows, tn), lambda j: (0, j)),
        out_shape=jax.ShapeDtypeStruct((rows, n), F32),
        compiler_params=_cparams(("arbitrary",)),
        name="mod",
    )(cs, w_ada, b_ada)


def _chunk_gates(pg):
    t = pg.shape[0]
    lane = lax.broadcasted_iota(jnp.int32, (1, pg.shape[1]), 1) % SUBLANES
    tt = lax.broadcasted_iota(jnp.int32, (t, t), 0)
    ss = lax.broadcasted_iota(jnp.int32, (t, t), 1)
    ltri = jnp.where(ss <= tt, 1.0, 0.0).astype(BF16)
    hi = pg.astype(BF16)
    r1 = pg - hi.astype(F32)
    mid = r1.astype(BF16)
    lo = (r1 - mid.astype(F32)).astype(BF16)
    cf = _dot(ltri, hi) + _dot(ltri, mid) + _dot(ltri, lo)
    rev = cf[t - 1:t, :] - cf + pg
    return jnp.where(lane == 1, cf, jnp.where(lane == 3, rev, pg))


def _inproj_kernel(n_conv, period, x_ref, mod_ref, g_ref, w_ref, b_ref, cw_ref, wg_ref, bg_ref,
                   o_ref, gr_ref, h_ref):
    tm = x_ref.shape[0]
    nj = w_ref.shape[0]
    h = _norm_mod(x_ref[...], g_ref[...], mod_ref[:, D_MODEL:2 * D_MODEL], mod_ref[:, 0:D_MODEL]).astype(BF16)
    h_ref[...] = h
    pg = _dot(h, wg_ref[...]) + bg_ref[...]
    lane = lax.broadcasted_iota(jnp.int32, (1, pg.shape[1]), 1) % SUBLANES
    log_f = jnp.minimum(pg, 0.0) - jnp.log(1.0 + jnp.exp(-jnp.abs(pg)))
    pg = jnp.where((lane == 1) | (lane == 3), log_f, pg)
    for c in range(tm // SCAN_CHUNK):
        cols = slice(c * SCAN_CHUNK, (c + 1) * SCAN_CHUNK)
        res_t = _chunk_gates(pg[cols]).T
        for hd in range(M_HEADS):
            gr_ref[hd, :, cols] = res_t[hd * SUBLANES:(hd + 1) * SUBLANES, :]

    nb = D_MODEL // CONV_COL_BLOCKS
    blocks = [(j, slice(n * nb, (n + 1) * nb)) for j in range(n_conv) for n in range(CONV_COL_BLOCKS)]
    pending = []

    def start(count):
        for _ in range(min(count, len(blocks))):
            j, cols = blocks.pop(0)
            pending.append((j, cols, _dot(h_ref[...], w_ref[j, :, cols]) + b_ref[j, :, cols]))

    def finish():
        while pending:
            j, cols, acc = pending.pop(0)
            y = _token_conv(acc, cw_ref[j, :, cols], period)
            o_ref[j, :, cols] = (y * _sigmoid(y)).astype(o_ref.dtype)

    start(len(blocks) - (nj - n_conv))
    for j in range(n_conv, nj):
        start(1)
        o_ref[j] = (_dot(h_ref[...], w_ref[j]) + b_ref[j]).astype(o_ref.dtype)
        finish()
    start(len(blocks))
    finish()


def _inproj(x2, mod3, mod_row_fn, g, w_t, b_t, cw_t, wg, bg, n_conv, period, tm):
    t, d = x2.shape
    nj = w_t.shape[0]
    const1 = lambda shape: pl.BlockSpec(shape, lambda i: (0,) * len(shape), pipeline_mode=pl.Buffered(1))
    return pl.pallas_call(
        functools.partial(_inproj_kernel, n_conv, period),
        grid=(t // tm,),
        in_specs=[pl.BlockSpec((tm, d), lambda i: (i, 0)),
                  pl.BlockSpec((None, 1, mod3.shape[2]), lambda i: (mod_row_fn(i), 0, 0)),
                  const1((1, d)), const1(w_t.shape), const1(b_t.shape), const1(cw_t.shape),
                  const1((d, LANES)), const1((1, LANES))],
        out_specs=[pl.BlockSpec((nj, tm, d), lambda i: (0, i, 0)),
                   pl.BlockSpec((M_HEADS, SUBLANES, tm), lambda i: (0, 0, i))],
        out_shape=[jax.ShapeDtypeStruct((nj, t, d), BF16),
                   jax.ShapeDtypeStruct((M_HEADS, SUBLANES, t), F32)],
        scratch_shapes=[pltpu.VMEM((tm, d), BF16)],
        compiler_params=_cparams(("parallel",)),
        name="inproj",
    )(x2, mod3, g, w_t, b_t, cw_t, wg, bg)


def _scan_step(q, k, v, gates, ctn_ref, m_ref, forward, want_out):
    t = k.shape[0]
    row0 = 0 if forward else 2
    i_row = gates[row0:row0 + 1, :]
    b_row = gates[row0 + 1:row0 + 2, :]
    b_last = b_row[:, t - 1:t] if forward else b_row[:, 0:1]
    m = m_ref[...]
    r = i_row - b_row
    m_new = b_last + jnp.maximum(m, jnp.max(r, axis=1, keepdims=True))
    k_t = k.T
    v1 = jnp.concatenate([v, jnp.ones((t, LANES), v.dtype)], axis=1)
    hout = None
    if want_out:
        tt = lax.broadcasted_iota(jnp.int32, (t, t), 0)
        ss = lax.broadcasted_iota(jnp.int32, (t, t), 1)
        allowed = (ss <= tt) if forward else (ss >= tt)
        m1 = jnp.where(allowed, r, NEG)
        cm = jnp.broadcast_to(jnp.maximum(m, jnp.max(m1, axis=1, keepdims=True)), (t, LANES))
        b_col = jnp.broadcast_to(jnp.sum(jnp.where(ss == tt, b_row, 0.0), axis=1, keepdims=True), (t, LANES))
        wide = lambda u, n: jnp.concatenate([u] * (n // LANES), axis=1)
        s = _dot(q, k_t) * jnp.exp(m1 - wide(cm, t))
        a = jnp.exp(m - cm)
        tot = _dot(s.astype(BF16), v1) + wide(a, HEAD_DIM + LANES) * _dot(q, ctn_ref[...].astype(BF16))
        den = tot[:, HEAD_DIM:]
        rden = 1.0 / jnp.maximum(jnp.abs(den), jnp.exp(-(b_col + cm)) * (HEAD_DIM ** 0.5))
        hout = tot[:, :HEAD_DIM] * wide(rden, HEAD_DIM)
    w_row = jnp.exp(b_last - m_new + r)
    decay = jnp.exp(b_last + m - m_new)
    kw = k_t * w_row.astype(BF16)
    ctn_ref[...] = decay * ctn_ref[...] + _dot(kw, v1)
    m_ref[...] = m_new
    return hout


def _mlstm_kernel(q_ref, k_ref, v_ref, o_ref, kc_ref, vc_ref, gr_ref, grc_ref, g_ref,
                  out_ref, hf_ref, hb_ref, ct_ref, m_ref):
    n_chunks = q_ref.shape[0] // SCAN_CHUNK
    t = SCAN_CHUNK
    hd = HEAD_DIM
    ct_ref[...] = jnp.zeros_like(ct_ref)
    m_ref[...] = jnp.zeros_like(m_ref)
    head_cols = [slice(h * hd, (h + 1) * hd) for h in range(MLSTM_HEADS)]

    for h, cols in enumerate(head_cols):
        for d, forward in enumerate((True, False)):
            _scan_step(None, kc_ref[:, cols], vc_ref[:, cols], grc_ref[h], ct_ref.at[h, d], m_ref.at[h, d],
                       forward, False)

    def body(c, carry):
        rf = pl.ds(pl.multiple_of(c * t, t), t)
        rb = pl.ds(pl.multiple_of((n_chunks - 1 - c) * t, t), t)
        for h, cols in enumerate(head_cols):
            hf_ref[rf, cols] = _scan_step(q_ref[rf, cols], k_ref[rf, cols], v_ref[rf, cols], gr_ref[h, :, rf],
                                          ct_ref.at[h, 0], m_ref.at[h, 0], True, True)
            hb_ref[rb, cols] = _scan_step(q_ref[rb, cols], k_ref[rb, cols], v_ref[rb, cols], gr_ref[h, :, rb],
                                          ct_ref.at[h, 1], m_ref.at[h, 1], False, True)
        return carry

    lax.fori_loop(0, n_chunks, body, 0)

    ones = jnp.ones((hd, LANES), BF16)
    wide = lambda u: jnp.concatenate([u] * (hd // LANES), axis=1)
    for cols in head_cols:
        hs = hf_ref[:, cols] + hb_ref[:, cols]
        mu = _dot(hs.astype(BF16), ones) * (1.0 / hd)
        dv = hs - wide(mu)
        var = _dot((dv * dv).astype(BF16), ones) * (1.0 / hd)
        y = dv * wide(lax.rsqrt(var + EPS)) * g_ref[:, cols] * _sigmoid(o_ref[:, cols].astype(F32))
        out_ref[:, cols] = y.astype(out_ref.dtype)


def _mlstm(streams, ctx_streams, gr, grc, m_norm_g, bsz, seq, ctx_len):
    hd = HEAD_DIM
    wd = MLSTM_HEADS * hd

    def stream(jidx):
        return pl.BlockSpec((None, seq, wd), lambda b, h: (jidx, b, h))

    def cstream(jidx):
        return pl.BlockSpec((None, ctx_len, wd), lambda b, h: (jidx, b, h))

    return pl.pallas_call(
        _mlstm_kernel,
        grid=(bsz, M_HEADS // MLSTM_HEADS),
        in_specs=[stream(J_Q), stream(J_K), stream(J_V), stream(J_O), cstream(0), cstream(1),
                  pl.BlockSpec((MLSTM_HEADS, SUBLANES, seq), lambda b, h: (h, 0, b)),
                  pl.BlockSpec((MLSTM_HEADS, SUBLANES, ctx_len), lambda b, h: (h, 0, b)),
                  pl.BlockSpec((1, wd), lambda b, h: (0, h))],
        out_specs=pl.BlockSpec((seq, wd), lambda b, h: (b, h)),
        out_shape=jax.ShapeDtypeStruct((bsz * seq, D_MODEL), BF16),
        scratch_shapes=[pltpu.VMEM((seq, wd), F32), pltpu.VMEM((seq, wd), F32),
                        pltpu.VMEM((MLSTM_HEADS, 2, hd, hd + LANES), F32),
                        pltpu.VMEM((MLSTM_HEADS, 2, 1, 1), F32)],
        compiler_params=_cparams(("parallel", "parallel")),
        name="mlstm",
    )(streams, streams, streams, streams, ctx_streams, ctx_streams, gr, grc, m_norm_g)


def _filt_kernel(z_ref, w1_ref, b1_ref, w2_ref, b2_ref, w3_ref, b3_ref, fr_ref, w4_ref, dec_ref, e_ref, d_ref):
    hp = lax.Precision.HIGHEST
    n = z_ref.shape[0]
    c = e_ref.shape[1]
    fr = fr_ref[...]
    a = jnp.sin(fr * (jnp.dot(z_ref[...], w1_ref[...], preferred_element_type=F32, precision=hp) + b1_ref[...]))
    a = jnp.sin(fr * (jnp.dot(a, w2_ref[...], preferred_element_type=F32, precision=hp) + b2_ref[...]))
    a = jnp.sin(fr * (jnp.dot(a, w3_ref[...], preferred_element_type=F32, precision=hp) + b3_ref[...]))
    taps = _dot(a.astype(BF16), w4_ref[...].astype(BF16))
    tcol = z_ref[:, 0:1]
    kf = taps[:, :c] * jnp.exp(-tcol * jnp.abs(dec_ref[0:1, :]))
    kb = taps[:, c:] * jnp.exp(-tcol * jnp.abs(dec_ref[1:2, :]))
    row0 = lax.broadcasted_iota(jnp.int32, (n, 1), 0) == 0
    hplus = jnp.where(row0, kf + kb, kf)
    hminus = jnp.where(row0, 0.0, kb)
    e_ref[...] = hplus + hminus
    d_ref[...] = hplus - hminus


def _filters(z, w1, b1, w2, b2, w3, b3, fr, w4, dec):
    n = z.shape[0]
    c = D_MODEL
    fw = w1.shape[1]
    full = lambda shape: pl.BlockSpec(shape, lambda o: (0,) * len(shape))
    return pl.pallas_call(
        _filt_kernel,
        grid=(HY_ORDER,),
        in_specs=[full(z.shape), full(w1.shape), full(b1.shape), full(w2.shape), full(b2.shape),
                  full(w3.shape), full(b3.shape), full(fr.shape),
                  pl.BlockSpec((fw, 2 * c), lambda o: (0, o)),
                  pl.BlockSpec((None, 2, c), lambda o: (o, 0, 0))],
        out_specs=[pl.BlockSpec((None, n, c), lambda o: (o, 0, 0)),
                   pl.BlockSpec((None, n, c), lambda o: (o, 0, 0))],
        out_shape=[jax.ShapeDtypeStruct((HY_ORDER, n, c), F32)] * 2,
        compiler_params=_cparams(("arbitrary",)),
        name="filt",
    )(z, w1, b1, w2, b2, w3, b3, fr, w4, dec)


FFT_R = 16
FFT_HALF = FFT_R // 2


def _axpy(acc, coef, x, memo):
    coef = round(coef, 15)
    if abs(coef) < 1e-12:
        return acc
    mag = abs(coef)
    if mag == 1.0:
        prod = x
    else:
        key = (id(x), mag)
        if key not in memo:
            memo[key] = (x, mag * x)
        prod = memo[key][1]
    if acc is None:
        return prod if coef > 0 else -prod
    return acc + prod if coef > 0 else acc - prod


def _stage_a_fwd(k1, zs, ssum, sdif, memo):
    even = k1 % 2 == 0
    re = _axpy(zs[0], math.cos(math.pi * k1 / 2), zs[FFT_HALF // 2], memo)
    im = _axpy(None, -math.sin(math.pi * k1 / 2), zs[FFT_HALF // 2], memo)
    for n in range(1, FFT_HALF // 2):
        th = 2 * math.pi * n * k1 / FFT_R
        re = _axpy(re, math.cos(th), ssum[n] if even else sdif[n], memo)
        im = _axpy(im, -math.sin(th), sdif[n] if even else ssum[n], memo)
    return re, im


class _StageAInverse:
    def __init__(self):
        self.y0 = None
        self.y4 = None
        self.t1 = [None] * (FFT_HALF // 2)
        self.t2 = [None] * (FFT_HALF // 2)

    def add(self, k1, vre, vim):
        even = k1 % 2 == 0
        memo = {}
        self.y0 = _axpy(self.y0, 1.0, vre, memo)
        self.y4 = _axpy(self.y4, math.cos(math.pi * k1 / 2), vre, memo)
        if vim is not None:
            self.y4 = _axpy(self.y4, -math.sin(math.pi * k1 / 2), vim, memo)
        for n in range(1, FFT_HALF // 2):
            th = 2 * math.pi * n * k1 / FFT_R
            if even:
                self.t1[n] = _axpy(self.t1[n], math.cos(th), vre, memo)
                if vim is not None:
                    self.t2[n] = _axpy(self.t2[n], -math.sin(th), vim, memo)
            else:
                self.t2[n] = _axpy(self.t2[n], math.cos(th), vre, memo)
                if vim is not None:
                    self.t1[n] = _axpy(self.t1[n], -math.sin(th), vim, memo)

    def slabs(self):
        ys = [None] * FFT_HALF
        ys[0], ys[FFT_HALF // 2] = self.y0, self.y4
        for n in range(1, FFT_HALF // 2):
            ys[n] = self.t1[n] + self.t2[n]
            ys[FFT_HALF - n] = self.t1[n] - self.t2[n]
        return ys


def _slabs(ref, m):
    zs = [ref[i * m:(i + 1) * m, :] for i in range(FFT_HALF)]
    ssum = [None] + [zs[n] + zs[FFT_HALF - n] for n in range(1, FFT_HALF // 2)]
    sdif = [None] + [zs[n] - zs[FFT_HALF - n] for n in range(1, FFT_HALF // 2)]
    return zs, ssum, sdif


def _stage_b_fwd(gb_ref, k1, re, im):
    m = re.shape[0]
    if im is None:
        return _dot(gb_ref[k1, :, 0:m], re.astype(BF16))
    return _dot(gb_ref[k1], jnp.concatenate([re, im], axis=0).astype(BF16))


def _spec_kernel(gb_ref, e_ref, d_ref, kre_ref, kim_ref):
    m = e_ref.shape[0] // FFT_HALF
    es = _slabs(e_ref, m)
    ds = _slabs(d_ref, m)
    memo_e, memo_d = {}, {}
    for k1 in range(FFT_HALF + 1):
        w = (1.0 if k1 in (0, FFT_HALF) else 2.0) / (FFT_R * m)
        kre_ref[k1] = (_stage_b_fwd(gb_ref, k1, *_stage_a_fwd(k1, *es, memo_e))[0:m] * w).astype(kre_ref.dtype)
        kim_ref[k1] = (_stage_b_fwd(gb_ref, k1, *_stage_a_fwd(k1, *ds, memo_d))[m:2 * m] * w).astype(kim_ref.dtype)


def _spectra(gb, e, d, ct):
    n = e.shape[1]
    c = e.shape[2]
    m = n // FFT_HALF
    tile = pl.BlockSpec((None, n, ct), lambda o, j: (o, 0, j))
    ktile = pl.BlockSpec((None, FFT_HALF + 1, m, ct), lambda o, j: (o, 0, 0, j))
    kshape = jax.ShapeDtypeStruct((HY_ORDER, FFT_HALF + 1, m, c), BF16)
    return pl.pallas_call(
        _spec_kernel,
        grid=(HY_ORDER, c // ct),
        in_specs=[pl.BlockSpec(gb.shape, lambda o, j: (0, 0, 0)), tile, tile],
        out_specs=[ktile, ktile],
        out_shape=[kshape, kshape],
        compiler_params=_cparams(("arbitrary", "arbitrary")),
        name="spec",
    )(gb, e, d)


def _hyena_kernel(gb_ref, hb_ref, kre_ref, kim_ref, skip_ref, cw_ref, v_ref, x1_ref, x2_ref, o_ref, z_ref):
    n = z_ref.shape[0]
    m = n // FFT_HALF
    z_ref[...] = _token_conv(v_ref[...], cw_ref[0].astype(BF16), GRID_W).astype(F32)
    gates = (x1_ref, x2_ref)
    for o in range(HY_ORDER):
        zs, ssum, sdif = _slabs(z_ref, m)
        inv = _StageAInverse()
        memo = {}
        for k1 in range(FFT_HALF + 1):
            re, im = _stage_a_fwd(k1, zs, ssum, sdif, memo)
            x = _stage_b_fwd(gb_ref, k1, re, im)
            xb = x.astype(BF16)
            xre, xim = xb[0:m], xb[m:2 * m]
            kre, kim = kre_ref[o, k1], kim_ref[o, k1]
            p = jnp.concatenate([xre * kre - xim * kim, xre * kim + xim * kre], axis=0)
            if im is None:
                inv.add(k1, _dot(hb_ref[k1, 0:m, :], p), None)
            else:
                v = _dot(hb_ref[k1], p)
                inv.add(k1, v[0:m], v[m:2 * m])
        ys = inv.slabs()
        gate = _token_conv(gates[o][...], cw_ref[o + 1].astype(BF16), GRID_W).astype(F32)
        for i in range(FFT_HALF):
            rows = slice(i * m, (i + 1) * m)
            z_ref[rows, :] = gate[rows] * (ys[i] + skip_ref[o:o + 1, :] * zs[i])
    o_ref[...] = z_ref[...].astype(o_ref.dtype)


def _hyena(gb, hb, kre, kim, skip, cw_h, streams, bsz, seq, ct):
    c = D_MODEL
    m = seq // FFT_HALF
    const1 = lambda shape: pl.BlockSpec(shape, lambda j, b: (0,) * len(shape), pipeline_mode=pl.Buffered(1))
    ktile = pl.BlockSpec((HY_ORDER, FFT_HALF + 1, m, ct), lambda j, b: (0, 0, 0, j), pipeline_mode=pl.Buffered(1))

    rows = seq

    def stream(jidx):
        return pl.BlockSpec((None, rows, ct), lambda j, b: (jidx, b, j))

    return pl.pallas_call(
        _hyena_kernel,
        grid=(c // ct, bsz),
        in_specs=[const1(gb.shape), const1(hb.shape), ktile, ktile,
                  pl.BlockSpec((HY_ORDER, ct), lambda j, b: (0, j)),
                  pl.BlockSpec((HY_ORDER + 1, SUBLANES, ct), lambda j, b: (0, 0, j)),
                  stream(J_HV), stream(J_HX1), stream(J_HX2)],
        out_specs=pl.BlockSpec((rows, ct), lambda j, b: (b, j)),
        out_shape=jax.ShapeDtypeStruct((bsz * seq, c), BF16),
        scratch_shapes=[pltpu.VMEM((rows, ct), F32)],
        compiler_params=_cparams(("parallel", "parallel")),
        name="hyena",
    )(gb, hb, kre, kim, skip, cw_h, streams, streams, streams)


def _tail_kernel(x_ref, mod_ref, hm_ref, hh_ref, gm_ref, gh_ref, wpm_ref, wph_ref, wout_ref,
                 n2_ref, w1_ref, w3_ref, w2_ref, fg_ref, o_ref):
    d = D_MODEL
    g1 = mod_ref[:, 2 * d:3 * d]
    sh2 = mod_ref[:, 3 * d:4 * d]
    sc2 = mod_ref[:, 4 * d:5 * d]
    g2 = mod_ref[:, 5 * d:6 * d]
    y = (_sigmoid(gm_ref[...].astype(F32)) * _dot(hm_ref[...], wpm_ref[...])
         + _sigmoid(gh_ref[...].astype(F32)) * _dot(hh_ref[...], wph_ref[...]))
    x1 = x_ref[...] + g1 * _dot(y.astype(BF16), wout_ref[...])
    h2 = _norm_mod(x1, n2_ref[...], sc2, sh2).astype(BF16)
    u = _dot(h2, w1_ref[...])
    a = (u * _sigmoid(u) * _dot(h2, w3_ref[...])).astype(BF16)
    x2 = x1 + g2 * _dot(a, w2_ref[...])
    o_ref[...] = x2 * lax.rsqrt(jnp.mean(x2 * x2, axis=-1, keepdims=True) + EPS) * fg_ref[...]


def _tail(x2d, mod3, hm, hh, streams, wpm, wph, wout, n2g, w1, w3, w2, fg, seq, tm):
    t, d = x2d.shape
    tiles_per_b = seq // tm
    const1 = lambda shape: pl.BlockSpec(shape, lambda i: (0,) * len(shape), pipeline_mode=pl.Buffered(1))
    rows = pl.BlockSpec((tm, d), lambda i: (i, 0))

    def stream(jidx):
        return pl.BlockSpec((None, tm, d), lambda i: (jidx, i, 0))

    return pl.pallas_call(
        _tail_kernel,
        grid=(t // tm,),
        in_specs=[rows, pl.BlockSpec((None, 1, mod3.shape[2]), lambda i: (i // tiles_per_b, 0, 0)),
                  rows, rows, stream(J_GM), stream(J_GH),
                  const1(wpm.shape), const1(wph.shape), const1(wout.shape), const1(n2g.shape),
                  const1(w1.shape), const1(w3.shape), const1(w2.shape), const1(fg.shape)],
        out_specs=rows,
        out_shape=jax.ShapeDtypeStruct((t, d), F32),
        compiler_params=_cparams(("parallel",)),
        name="tail",
    )(x2d, mod3, hm, hh, streams, streams, wpm, wph, wout, n2g, w1, w3, w2, fg)


def _stage_b_tables(n):
    m = n // FFT_HALF
    k1 = lax.broadcasted_iota(jnp.int32, (FFT_HALF + 1, m, m), 0)
    k2 = lax.broadcasted_iota(jnp.int32, (FFT_HALF + 1, m, m), 1)
    n2 = lax.broadcasted_iota(jnp.int32, (FFT_HALF + 1, m, m), 2)
    ang = ((n2 * (k1 + FFT_R * k2)) % (2 * n)).astype(F32) * (math.pi / n)
    cs, sn = jnp.cos(ang), jnp.sin(ang)
    gb = jnp.concatenate([jnp.concatenate([cs, sn], axis=2), jnp.concatenate([-sn, cs], axis=2)], axis=1)
    return gb.astype(BF16), jnp.swapaxes(gb, 1, 2).astype(BF16)


def _pos_embedding(n):
    t = jnp.linspace(0.0, 1.0, n, dtype=F32)[:, None]
    bands = jnp.linspace(1e-4, HY_BANDS - 1, HY_BANDS, dtype=F32)
    ang = (2.0 * math.pi / n) * jnp.arange(n, dtype=F32)[:, None] * bands[None, :]
    return jnp.concatenate([t, jnp.cos(ang), -jnp.sin(ang)], axis=-1)


def kernel(x, c, ctx, c_ctx, w_ada, b_ada, norm1_g, w_in, b_in, kq_conv_w, kq_conv_b, m_norm_g, hy_conv_w, hy_conv_b, hy_w1, hy_b1, hy_w2, hy_b2, hy_w3, hy_b3, hy_w4, hy_freq, hy_decay, hy_skip, w_pm, w_ph, w_out, norm2_g, ffn_w1, ffn_w3, ffn_w2, final_g):
    assert w_ada.shape[0] == 1, "single trunk layer"
    bsz, seq, d = x.shape
    ctx_len = ctx.shape[1]
    assert d == D_MODEL and seq % SCAN_CHUNK == 0 and ctx_len == SCAN_CHUNK
    row = lambda v: v.reshape(1, -1)

    wi, bi = w_in[0], b_in[0]
    col = lambda off: slice(off, off + d)
    order = [col(0), col(Q_OFF), col(HY_OFF), col(HY_OFF + d), col(HY_OFF + 2 * d),
             col(d), col(O_OFF), col(MG_OFF), col(MG_OFF + d)]
    w_t = jnp.stack([wi[:, s] for s in order]).astype(BF16)
    b_t = jnp.stack([bi[s] for s in order])[:, None, :]
    kqw, kqb, hyw, hyb = kq_conv_w[0], kq_conv_b[0], hy_conv_w[0], hy_conv_b[0]
    ones = jnp.ones((d,), F32)
    zeros = jnp.zeros((SUBLANES - 5, d), F32)

    def conv_rows(w3, cb, scale):
        return jnp.concatenate([w3, cb[None], (scale * ones)[None], zeros], axis=0)

    cw_kq = jnp.stack([conv_rows(kqw[:, :d], kqb[:d], 1.0), conv_rows(kqw[:, d:], kqb[d:], 1.0)])
    cw_h = jnp.stack([conv_rows(hyw[:, col(i * d)], hyb[col(i * d)], 1.0) for i in range(3)])
    ctx_sel = jnp.array([J_K, J_V])
    gsrc = jnp.array([G_OFF + g * M_HEADS + h for h in range(M_HEADS) for g in range(4)])
    gdst = jnp.array([SUBLANES * h + g for h in range(M_HEADS) for g in range(4)])
    wg = jnp.zeros((d, LANES), F32).at[:, gdst].set(wi[:, gsrc]).astype(BF16)
    bg = jnp.zeros((1, LANES), F32).at[0, gdst].set(bi[gsrc])

    n_mod = ((bsz + 1 + 7) // 8) * 8
    cs = jnp.zeros((n_mod, d), F32).at[:bsz].set(c).at[bsz].set(c_ctx)
    mod3 = _mod(cs, w_ada[0], row(b_ada[0]))[:, None, :]

    x2 = x.reshape(bsz * seq, d)
    ctx2 = ctx.reshape(bsz * ctx_len, d)
    g1n = row(norm1_g[0])
    tm = min(INPROJ_ROWS, seq)
    tmc = min(INPROJ_ROWS, bsz * ctx_len)
    lat_row = lambda i: i // (seq // tm)
    ctx_row = lambda i: bsz

    streams, gr = _inproj(x2, mod3, lat_row, g1n, w_t, b_t, cw_kq, wg, bg, N_CONV_TILES, GRID_W, tm)
    ctx_streams, grc = _inproj(ctx2, mod3, ctx_row, g1n, w_t[ctx_sel], b_t[ctx_sel], cw_kq[:1], wg, bg,
                               1, ctx_len, tmc)
    hm = _mlstm(streams, ctx_streams, gr, grc, row(m_norm_g[0]), bsz, seq, ctx_len)

    zemb = jnp.pad(_pos_embedding(seq), ((0, 0), (0, 7)))
    w1p = jnp.pad(hy_w1[0], ((0, 7), (0, 0)))
    e, dd = _filters(zemb, w1p, row(hy_b1[0]), hy_w2[0], row(hy_b2[0]), hy_w3[0], row(hy_b3[0]),
                     row(hy_freq[0]), hy_w4[0], hy_decay[0])
    gb, hb = _stage_b_tables(seq)
    kre, kim = _spectra(gb, e, dd, HY_CHANNELS)
    hh = _hyena(gb, hb, kre, kim, hy_skip[0], cw_h, streams, bsz, seq, HY_CHANNELS)

    out = _tail(x2, mod3, hm, hh, streams, w_pm[0].astype(BF16), w_ph[0].astype(BF16), w_out[0].astype(BF16),
                row(norm2_g[0]), ffn_w1[0].astype(BF16), ffn_w3[0].astype(BF16), ffn_w2[0].astype(BF16),
                row(final_g), seq, min(TAIL_ROWS, seq))
    return out.reshape(bsz, seq, d)
```
